```python
import jax, jax.numpy as jnp
from jax import lax
import numpy as np

D_MODEL = 2048
BATCH = 4
SEQ = 2048
DEPTH = 1
DEC_BATCH = 128
DEC_SEQ = 1
PAST_LEN = 16384
PAGE_SIZE = 128

GDN_DK = 128
GDN_DV = 128
GDN_HEADS = D_MODEL // GDN_DV
GDN_CONV = 4
GDN_CHUNK = 64
HG_DK = 128
HG_DV = 128
HG_HEADS = D_MODEL // HG_DV
HG_CHUNK = 32
N_EXPERTS = 32
TOP_K = 4
D_EXPERT = D_MODEL
SWIGLU_LIMIT = 7.0
SWIGLU_ALPHA = 1.702
MOE_BLOCK = 128
DN_ALPHA = (2.0 * DEPTH) ** 0.25
DN_BETA = (8.0 * DEPTH) ** -0.25
LN_EPS = 1e-5
RMS_EPS = 1e-6
L2_EPS = 1e-6

GDN_QK = GDN_HEADS * GDN_DK
GDN_V = GDN_HEADS * GDN_DV
HG_K = HG_HEADS * HG_DK
HG_V = HG_HEADS * HG_DV
CONV_CH = 2 * GDN_QK + GDN_V
IN_SPLITS = (GDN_QK, GDN_QK, GDN_V, GDN_HEADS, GDN_HEADS, GDN_V,
             HG_K, HG_K, HG_V, HG_V, D_MODEL, D_MODEL)
D_IN = sum(IN_SPLITS)

kernel_name = "hybrid_gdn_hgrn2_moe_deepnorm_step"


def _split_points():
    return np.cumsum(np.array(IN_SPLITS))[:-1].tolist()


def _layer_norm(x, g, b):
    mu = jnp.mean(x, axis=-1, keepdims=True)
    xc = x - mu
    var = jnp.mean(xc * xc, axis=-1, keepdims=True)
    return xc * lax.rsqrt(var + LN_EPS) * g + b


def _rms_heads(o, w):
    return o * lax.rsqrt(jnp.mean(o * o, axis=-1, keepdims=True) + RMS_EPS) * w


def _l2norm(x):
    return x * lax.rsqrt(jnp.sum(x * x, axis=-1, keepdims=True) + L2_EPS)


def _causal_conv_silu(u, buf, w):
    L = u.shape[1]
    full = jnp.concatenate([buf, u], axis=1)
    out = sum(full[:, j:j + L] * w[j] for j in range(GDN_CONV))
    return jax.nn.silu(out), full[:, L:]


def _chunking(L, chunk):
    C = min(chunk, L)
    n = -(-L // C)
    return C, n


def _pad_time(t, Lp):
    pad = Lp - t.shape[1]
    if pad == 0:
        return t
    widths = [(0, 0)] * t.ndim
    widths[1] = (0, pad)
    return jnp.pad(t, widths)


def _to_chunks(t, n, C):
    B, _, H = t.shape[:3]
    t = t.reshape((B, n, C, H) + t.shape[3:])
    return t.transpose((1, 0, 3, 2) + tuple(range(4, t.ndim)))


def _from_chunks(t, L):
    n, B, H, C, dv = t.shape
    return t.transpose(1, 0, 3, 2, 4).reshape(B, n * C, H, dv)[:, :L]


def _gated_delta_chunked(q, k, v, g, beta, s0):
    L = q.shape[1]
    dk = q.shape[-1]
    C, n = _chunking(L, GDN_CHUNK)
    Lp = n * C
    q, k, v, g, beta = [_to_chunks(_pad_time(t, Lp), n, C) for t in (q, k, v, g, beta)]
    gc = jnp.cumsum(g, axis=-1)
    incl = jnp.tril(jnp.ones((C, C), dtype=bool))
    strict = jnp.tril(jnp.ones((C, C), dtype=bool), -1)
    decay = jnp.exp(jnp.where(incl, gc[..., :, None] - gc[..., None, :], -jnp.inf))
    kk = jnp.einsum('nbhtk,nbhsk->nbhts', k, k)
    m = jnp.where(strict, beta[..., :, None] * decay * kk, 0.0) + jnp.eye(C, dtype=jnp.float32)
    rhs = jnp.concatenate([(beta * jnp.exp(gc))[..., None] * k, beta[..., None] * v], axis=-1)
    sol = lax.linalg.triangular_solve(m, rhs, left_side=True, lower=True, unit_diagonal=True)
    w_c, u_c = sol[..., :dk], sol[..., dk:]
    a_qk = jnp.einsum('nbhtk,nbhsk->nbhts', q, k) * decay
    q_dec = q * jnp.exp(gc)[..., None]
    k_end = k * jnp.exp(gc[..., -1:] - gc)[..., None]
    g_end = jnp.exp(gc[..., -1])

    def step(s, xs):
        w_i, u_i, a_i, qd_i, ke_i, ge_i = xs
        u = u_i - jnp.einsum('bhtk,bhkv->bhtv', w_i, s)
        o = jnp.einsum('bhtk,bhkv->bhtv', qd_i, s) + jnp.einsum('bhts,bhsv->bhtv', a_i, u)
        s = ge_i[..., None, None] * s + jnp.einsum('bhsk,bhsv->bhkv', ke_i, u)
        return s, o

    s_fin, o = lax.scan(step, s0, (w_c, u_c, a_qk, q_dec, k_end, g_end))
    return _from_chunks(o, L), s_fin


def _hgrn2_chunked(q, k, logf, v, s0):
    L = q.shape[1]
    C, n = _chunking(L, HG_CHUNK)
    Lp = n * C
    q, k, logf, v = [_to_chunks(_pad_time(t, Lp), n, C) for t in (q, k, logf, v)]
    incl = jnp.tril(jnp.ones((C, C), dtype=bool))[:, :, None]

    def step(s, xs):
        qc, kc, lfc, vc = xs
        b = jnp.cumsum(lfc, axis=-2)
        decay = jnp.exp(jnp.where(incl, b[..., :, None, :] - b[..., None, :, :], -jnp.inf))
        attn = jnp.sum(qc[..., :, None, :] * decay * kc[..., None, :, :], axis=-1)
        o = (jnp.einsum('bhtk,bhkv->bhtv', qc * jnp.exp(b), s)
             + jnp.einsum('bhts,bhsv->bhtv', attn, vc))
        s = (jnp.exp(b[..., -1, :])[..., None] * s
             + jnp.einsum('bhsk,bhsv->bhkv', kc * jnp.exp(b[..., -1:, :] - b), vc))
        return s, o

    s_fin, o = lax.scan(step, s0, (q, k, logf, v))
    return _from_chunks(o, L), s_fin


def _token_mixer(x, conv_buf, s_gdn, s_hg, lb, w_in, conv_w, a_log, dt_bias,
                 gdn_norm_w, hg_norm_w, w_out):
    f32 = jnp.float32
    B, L, _ = x.shape
    proj = jnp.einsum('bld,de->ble', x, w_in).astype(f32)
    (q_a, k_a, v_a, a_a, b_a, z_a, q_b, f_b, i_b, z_b, r_a, r_b) = jnp.split(
        proj, _split_points(), axis=-1)
    qkv, new_buf = _causal_conv_silu(jnp.concatenate([q_a, k_a, v_a], axis=-1),
                                     conv_buf.astype(f32), conv_w.astype(f32))
    q_a, k_a, v_a = jnp.split(qkv, [GDN_QK, 2 * GDN_QK], axis=-1)
    q_a = _l2norm(q_a.reshape(B, L, GDN_HEADS, GDN_DK)) * (GDN_DK ** -0.5)
    k_a = _l2norm(k_a.reshape(B, L, GDN_HEADS, GDN_DK))
    v_a = v_a.reshape(B, L, GDN_HEADS, GDN_DV)
    beta = jax.nn.sigmoid(b_a)
    g = -jnp.exp(a_log.astype(f32)) * jax.nn.softplus(a_a + dt_bias.astype(f32))
    o_a, s_gdn_new = _gated_delta_chunked(q_a, k_a, v_a, g, beta, s_gdn.astype(f32))
    o_a = (_rms_heads(o_a, gdn_norm_w.astype(f32))
           * jax.nn.silu(z_a.reshape(B, L, GDN_HEADS, GDN_DV))).reshape(B, L, GDN_V)
    f = lb + (1.0 - lb) * jax.nn.sigmoid(f_b)
    q_h = (jax.nn.silu(q_b) * (HG_DK ** -0.5)).reshape(B, L, HG_HEADS, HG_DK)
    k_h = (1.0 - f).reshape(B, L, HG_HEADS, HG_DK)
    lf_h = jnp.log(f).reshape(B, L, HG_HEADS, HG_DK)
    i_h = i_b.reshape(B, L, HG_HEADS, HG_DV)
    o_b, s_hg_new = _hgrn2_chunked(q_h, k_h, lf_h, i_h, s_hg.astype(f32))
    o_b = (_rms_heads(o_b, hg_norm_w.astype(f32))
           * jax.nn.silu(z_b.reshape(B, L, HG_HEADS, HG_DV))).reshape(B, L, HG_V)
    merged = jax.nn.sigmoid(r_a) * o_a + jax.nn.sigmoid(r_b) * o_b
    out = jnp.einsum('bld,de->ble', merged, w_out).astype(f32)
    return out, new_buf, s_gdn_new, s_hg_new


def _moe(x2, w_router, b_router, w_gu, b_gu, w_dn, b_dn):
    f32 = jnp.float32
    T, D = x2.shape
    logits = (x2 @ w_router + b_router).astype(f32)
    top_v, top_i = lax.top_k(logits, TOP_K)
    gates = jax.nn.softmax(top_v, axis=-1)
    flat_e = top_i.reshape(-1)
    flat_t = jnp.repeat(jnp.arange(T, dtype=jnp.int32), TOP_K)
    flat_g = gates.reshape(-1)
    order = jnp.argsort(flat_e)
    se = flat_e[order]
    counts = jnp.bincount(flat_e, length=N_EXPERTS)
    starts = jnp.cumsum(counts) - counts
    pcounts = (counts + MOE_BLOCK - 1) // MOE_BLOCK * MOE_BLOCK
    pends = jnp.cumsum(pcounts)
    pstarts = pends - pcounts
    dest = pstarts[se] + (jnp.arange(T * TOP_K) - starts[se])
    n_blocks = -(-(T * TOP_K) // MOE_BLOCK) + N_EXPERTS
    P = n_blocks * MOE_BLOCK
    tok_buf = jnp.full((P,), T, dtype=jnp.int32).at[dest].set(flat_t[order])
    gate_buf = jnp.zeros((P,), f32).at[dest].set(flat_g[order])
    block_e = jnp.minimum(
        jnp.searchsorted(pends, jnp.arange(n_blocks) * MOE_BLOCK, side='right'), N_EXPERTS - 1)
    x_pad = jnp.concatenate([x2, jnp.zeros((1, D), x2.dtype)], axis=0)
    xb = x_pad[tok_buf].reshape(n_blocks, MOE_BLOCK, D)

    def expert_block(args):
        xblk, e = args
        h = (xblk @ w_gu[e] + b_gu[e]).astype(f32)
        gate = jnp.minimum(h[:, ::2], SWIGLU_LIMIT)
        up = jnp.clip(h[:, 1::2], -SWIGLU_LIMIT, SWIGLU_LIMIT)
        act = (up + 1.0) * (gate * jax.nn.sigmoid(gate * SWIGLU_ALPHA))
        return (act @ w_dn[e] + b_dn[e]).astype(f32)

    yb = lax.map(expert_block, (xb, block_e)).reshape(P, D)
    out = jnp.zeros((T + 1, D), f32).at[tok_buf].add(yb * gate_buf[:, None])
    return out[:T]


def _decoder_layer(x, conv_buf, s_gdn, s_hg, lb, w_in, conv_w, a_log, dt_bias, gdn_norm_w,
                   hg_norm_w, w_out, ln1_g, ln1_b, w_router, b_router, w_gu, b_gu, w_dn, b_dn,
                   ln2_g, ln2_b):
    f32 = jnp.float32
    mix, new_buf, s_gdn_new, s_hg_new = _token_mixer(
        x, conv_buf, s_gdn, s_hg, lb, w_in, conv_w, a_log, dt_bias, gdn_norm_w, hg_norm_w, w_out)
    h = _layer_norm(DN_ALPHA * x + mix, ln1_g.astype(f32), ln1_b.astype(f32))
    B, L, D = h.shape
    ffn = _moe(h.reshape(B * L, D), w_router, b_router, w_gu, b_gu, w_dn, b_dn).reshape(B, L, D)
    y = _layer_norm(DN_ALPHA * h + ffn, ln2_g.astype(f32), ln2_b.astype(f32))
    return y, new_buf, s_gdn_new, s_hg_new


def setup_inputs(seed: int = 0) -> dict:
    key = jax.random.key(seed)
    ks = jax.random.split(key, 32)
    nrm = jax.random.normal
    f32 = jnp.float32
    dt = jax.random.uniform(ks[6], (DEPTH, GDN_HEADS), f32, 1e-3, 1e-1)
    return {
        "x_prompt": nrm(ks[0], (BATCH, SEQ, D_MODEL), f32),
        "x_sample": nrm(ks[1], (DEC_BATCH, DEC_SEQ, D_MODEL), f32),
        "state_gdn_conv": nrm(ks[2], (DEPTH, DEC_BATCH, GDN_CONV - 1, CONV_CH), f32) * 0.5,
        "state_gdn_s": nrm(ks[3], (DEPTH, DEC_BATCH, GDN_HEADS, GDN_DK, GDN_DV), f32) * 0.3,
        "state_hgrn_s": nrm(ks[4], (DEPTH, DEC_BATCH, HG_HEADS, HG_DK, HG_DV), f32) * 0.3,
        "w_in": nrm(ks[5], (DEPTH, D_MODEL, D_IN), f32) * D_MODEL ** -0.5,
        "gdn_conv_w": nrm(ks[7], (DEPTH, GDN_CONV, CONV_CH), f32) * GDN_CONV ** -0.5,
        "gdn_a_log": jnp.log(jax.random.uniform(ks[8], (DEPTH, GDN_HEADS), f32, 1.0, 16.0)),
        "gdn_dt_bias": dt + jnp.log(-jnp.expm1(-dt)),
        "gdn_norm_w": 1.0 + 0.02 * nrm(ks[9], (DEPTH, GDN_DV), f32),
        "hg_lb_logits": 0.1 * nrm(ks[10], (DEPTH + 1, HG_K), f32),
        "hg_norm_w": 1.0 + 0.02 * nrm(ks[11], (DEPTH, HG_DV), f32),
        "w_out": nrm(ks[12], (DEPTH, D_MODEL, D_MODEL), f32) * (D_MODEL ** -0.5) * DN_BETA,
        "ln1_g": 1.0 + 0.02 * nrm(ks[13], (DEPTH, D_MODEL), f32),
        "ln1_b": 0.02 * nrm(ks[14], (DEPTH, D_MODEL), f32),
        "w_router": nrm(ks[15], (DEPTH, D_MODEL, N_EXPERTS), f32) * D_MODEL ** -0.5,
        "b_router": 0.01 * nrm(ks[16], (DEPTH, N_EXPERTS), f32),
        "w_gate_up": nrm(ks[17], (DEPTH, N_EXPERTS, D_MODEL, 2 * D_EXPERT), f32) * D_MODEL ** -0.5,
        "b_gate_up": 0.01 * nrm(ks[18], (DEPTH, N_EXPERTS, 2 * D_EXPERT), f32),
        "w_down": nrm(ks[19], (DEPTH, N_EXPERTS, D_EXPERT, D_MODEL), f32) * (D_EXPERT ** -0.5) * DN_BETA,
        "b_down": 0.01 * nrm(ks[20], (DEPTH, N_EXPERTS, D_MODEL), f32),
        "ln2_g": 1.0 + 0.02 * nrm(ks[21], (DEPTH, D_MODEL), f32),
        "ln2_b": 0.02 * nrm(ks[22], (DEPTH, D_MODEL), f32),
    }


def reference(x_prompt, x_sample, state_gdn_conv, state_gdn_s, state_hgrn_s, w_in, gdn_conv_w,
              gdn_a_log, gdn_dt_bias, gdn_norm_w, hg_lb_logits, hg_norm_w, w_out, ln1_g, ln1_b,
              w_router, b_router, w_gate_up, b_gate_up, w_down, b_down, ln2_g, ln2_b):
    f32 = jnp.float32
    lb_all = jnp.cumsum(jax.nn.softmax(hg_lb_logits.astype(f32), axis=0), axis=0)
    xp = x_prompt.astype(f32)
    xs = x_sample.astype(f32)
    bp = xp.shape[0]
    conv_p, gdn_p, hg_p, conv_s, gdn_s, hg_s = [], [], [], [], [], []
    for l in range(DEPTH):
        lw = (w_in[l], gdn_conv_w[l], gdn_a_log[l], gdn_dt_bias[l], gdn_norm_w[l], hg_norm_w[l],
              w_out[l], ln1_g[l], ln1_b[l], w_router[l], b_router[l], w_gate_up[l], b_gate_up[l],
              w_down[l], b_down[l], ln2_g[l], ln2_b[l])
        xp, c_new, g_new, h_new = _decoder_layer(
            xp, jnp.zeros((bp, GDN_CONV - 1, CONV_CH), f32),
            jnp.zeros((bp, GDN_HEADS, GDN_DK, GDN_DV), f32),
            jnp.zeros((bp, HG_HEADS, HG_DK, HG_DV), f32), lb_all[l], *lw)
        conv_p.append(c_new); gdn_p.append(g_new); hg_p.append(h_new)
        xs, c_new, g_new, h_new = _decoder_layer(
            xs, state_gdn_conv[l], state_gdn_s[l], state_hgrn_s[l], lb_all[l], *lw)
        conv_s.append(c_new); gdn_s.append(g_new); hg_s.append(h_new)
    y_prompt = xp.astype(x_prompt.dtype)
    y_sample = xs.astype(x_sample.dtype)
    new_conv_p = jnp.stack(conv_p).astype(state_gdn_conv.dtype)
    new_gdn_p = jnp.stack(gdn_p).astype(state_gdn_s.dtype)
    new_hg_p = jnp.stack(hg_p).astype(state_hgrn_s.dtype)
    new_conv_s = jnp.stack(conv_s).astype(state_gdn_conv.dtype)
    new_gdn_s = jnp.stack(gdn_s).astype(state_gdn_s.dtype)
    new_hg_s = jnp.stack(hg_s).astype(state_hgrn_s.dtype)
    return (y_prompt, y_sample, new_conv_p, new_gdn_p, new_hg_p, new_conv_s, new_gdn_s, new_hg_s)
```

```python
import functools

import jax
import jax.numpy as jnp
from jax import lax
from jax.experimental import pallas as pl
from jax.experimental.pallas import tpu as pltpu

F32 = jnp.float32
BF16 = jnp.bfloat16
I32 = jnp.int32
U32 = jnp.uint32

HEAD_DIM = 128
LANES = 128
GDN_CHUNK = 64
HG_SUB = 16
TIME_BLOCK = 256
TOP_K = 4
MOE_ROW_BLOCK = 256
MOE_ROW_TILE = 1024
MOE_COL_TILE = 512
SWIGLU_LIMIT = 7.0
SWIGLU_ALPHA = 1.702
LN_EPS = 1e-5
RMS_EPS = 1e-6
L2_EPS = 1e-6
VMEM_LIMIT = 56 * 1024 * 1024


def _params(n_axes, vmem=VMEM_LIMIT):
    return pltpu.CompilerParams(dimension_semantics=("arbitrary",) * n_axes,
                                vmem_limit_bytes=vmem)


def _pick(n, target, mult):
    best = None
    for d in range(mult, min(n, target) + 1, mult):
        if n % d == 0:
            best = d
    assert best is not None, (n, target, mult)
    return best


def _dot(a, b):
    return jnp.dot(a, b, preferred_element_type=F32)


def _dot_nt(a, b):
    return lax.dot_general(a, b, (((1,), (1,)), ((), ())), preferred_element_type=F32)


def _dot_tn(a, b):
    return lax.dot_general(a, b, (((0,), (0,)), ((), ())), preferred_element_type=F32)


def _hi_lo(x):
    hi = x.astype(BF16)
    lo = (x - hi.astype(F32)).astype(BF16)
    return hi, lo


def _dot3(a, b):
    ah, al = _hi_lo(a)
    bh, bl = _hi_lo(b)
    return _dot(ah, bh) + (_dot(ah, bl) + _dot(al, bh))


def _dot_exact_lhs(m_bf16, x):
    p1 = x.astype(BF16)
    r1 = x - p1.astype(F32)
    p2 = r1.astype(BF16)
    p3 = (r1 - p2.astype(F32)).astype(BF16)
    return _dot(m_bf16, p1) + (_dot(m_bf16, p2) + _dot(m_bf16, p3))


def _sigmoid(x):
    return 1.0 / (1.0 + jnp.exp(-x))


def _silu(x):
    return x * _sigmoid(x)


def _softplus(x):
    return jnp.maximum(x, 0.0) + jnp.log(1.0 + jnp.exp(-jnp.abs(x)))


def _iota(shape, dim):
    return lax.broadcasted_iota(I32, shape, dim)


def _layer_norm(y, g, b):
    mu = jnp.mean(y, axis=-1, keepdims=True)
    yc = y - mu
    var = jnp.mean(yc * yc, axis=-1, keepdims=True)
    return yc * lax.rsqrt(var + LN_EPS) * g + b


def _rms_gate(o, w, z):
    return o * lax.rsqrt(jnp.mean(o * o, axis=-1, keepdims=True) + RMS_EPS) * w * _silu(z)


def _l2norm(x):
    return x * lax.rsqrt(jnp.sum(x * x, axis=-1, keepdims=True) + L2_EPS)


def _lane_pick(x, idx):
    lane = _iota(x.shape, 1)
    return jnp.sum(jnp.where(lane == idx, x, 0.0), axis=1, keepdims=True)


def _mm_kernel(x_ref, w_ref, o_ref):
    o_ref[...] = _dot(x_ref[...].astype(BF16), w_ref[...])


def _in_projection(x, w):
    n, d = x.shape
    no = w.shape[1]
    tm = _pick(n, 1040, 16)
    tn = _pick(no, 1024, LANES)
    return pl.pallas_call(
        _mm_kernel,
        grid=(n // tm, no // tn),
        in_specs=[pl.BlockSpec((tm, d), lambda i, j: (i, 0)),
                  pl.BlockSpec((d, tn), lambda i, j: (0, j))],
        out_specs=pl.BlockSpec((tm, tn), lambda i, j: (i, j)),
        out_shape=jax.ShapeDtypeStruct((n, no), F32),
        compiler_params=_params(2),
        name="in_projection",
    )(x, w)


def _gates_kernel(x_ref, w_ref, prm_ref, g_ref, gt_ref, *, n_prompt_tiles, n_heads):
    i = pl.program_id(0)
    ab = _dot3(x_ref[...], w_ref[...])
    prm = prm_ref[...]
    g = -jnp.exp(prm[0:1]) * _softplus(ab + prm[1:2])
    beta = _sigmoid(ab)
    tm = ab.shape[0]
    r = _iota((tm, tm), 0)
    c = _iota((tm, tm), 1)
    shift = jnp.where(i < n_prompt_tiles, GDN_CHUNK.bit_length() - 1, 0)
    tri = jnp.logical_and((r >> shift) == (c >> shift), r >= c)
    gc = _dot_exact_lhs(jnp.where(tri, 1.0, 0.0).astype(BF16), g)
    lane = _iota(ab.shape, 1)
    out = jnp.where(lane < n_heads, gc, beta)
    g_ref[...] = out
    gt_ref[...] = out.T


def _gates(x, w_ab, prm, n_prompt, n_heads):
    n, d = x.shape
    tm = LANES
    assert n % tm == 0 and n_prompt % tm == 0 and tm % GDN_CHUNK == 0
    kern = functools.partial(_gates_kernel, n_prompt_tiles=n_prompt // tm, n_heads=n_heads)
    return pl.pallas_call(
        kern,
        grid=(n // tm,),
        in_specs=[pl.BlockSpec((tm, d), lambda i: (i, 0)),
                  pl.BlockSpec((d, LANES), lambda i: (0, 0)),
                  pl.BlockSpec((8, LANES), lambda i: (0, 0))],
        out_specs=[pl.BlockSpec((tm, LANES), lambda i: (i, 0)),
                   pl.BlockSpec((LANES, tm), lambda i: (0, i))],
        out_shape=[jax.ShapeDtypeStruct((n, LANES), F32),
                   jax.ShapeDtypeStruct((LANES, n), F32)],
        compiler_params=_params(1),
        name="gates",
    )(x, w_ab, prm)


def _unit_lower_inverse(a):
    c = a.shape[0]
    r = _iota((c, c), 0)
    col = _iota((c, c), 1)
    inv = jnp.where(r == col, 1.0, 0.0) - jnp.where(
        jnp.logical_and((r >> 1) == (col >> 1), r > col), a, 0.0)
    level = 2
    while (1 << level) <= c:
        half = level - 1
        mask = jnp.logical_and(
            (r >> level) == (col >> level),
            jnp.logical_and(((r >> half) & 1) == 1, ((col >> half) & 1) == 0))
        low = jnp.where(mask, a, 0.0)
        inv = inv - _dot3(inv, _dot3(low, inv))
        level += 1
    return inv


def _gdn_chunk(q, k, v, gcc, gcr, bc, s):
    c = q.shape[0]
    r = _iota((c, c), 0)
    col = _iota((c, c), 1)
    incl = r >= col
    decay = jnp.where(incl, jnp.exp(jnp.where(incl, gcc - gcr, 0.0)), 0.0)
    kb = k.astype(BF16)
    qb = q.astype(BF16)
    sb = s.astype(BF16)
    a = jnp.where(r > col, bc * decay * _dot_nt(kb, kb), 0.0)
    inv = _unit_lower_inverse(a)
    egc = jnp.exp(gcc)
    rhs = bc * (v - egc * _dot(kb, sb))
    u = _dot3(inv, rhs)
    ub = u.astype(BF16)
    aqk = _dot_nt(qb, kb) * decay
    o = _dot((q * egc).astype(BF16), sb) + _dot(aqk.astype(BF16), ub)
    g_last = gcc[c - 1:c, :]
    ke = k * jnp.exp(g_last - gcc)
    s_new = jnp.exp(g_last) * s + _dot_tn(ke.astype(BF16), ub)
    return o, s_new


def _gdn_prompt_kernel(pq_ref, pk_ref, pv_ref, pz_ref, g_ref, gt_ref, wq_ref, wk_ref, wv_ref,
                       nw_ref, o_ref, s_ref, state, cbuf, qs, ks, vs):
    h = pl.program_id(1)
    t = pl.program_id(2)
    tb = pq_ref.shape[0]

    @pl.when(t == 0)
    def _():
        state[...] = jnp.zeros_like(state)
        cbuf[:, 0:8, :] = jnp.zeros((3, 8, HEAD_DIM), F32)

    def conv(idx, u_ref, w_ref):
        u = u_ref[...]
        cbuf[idx, 8:8 + tb, :] = u
        w = w_ref[...]
        y = cbuf[idx, 5:5 + tb, :] * w[0:1, :]
        for j in range(1, 4):
            y = y + cbuf[idx, 5 + j:5 + j + tb, :] * w[j:j + 1, :]
        cbuf[idx, 0:8, :] = u[tb - 8:tb, :]
        return _silu(y)

    qs[...] = _l2norm(conv(0, pq_ref, wq_ref)) * (HEAD_DIM ** -0.5)
    ks[...] = _l2norm(conv(1, pk_ref, wk_ref))
    vs[...] = conv(2, pv_ref, wv_ref)

    gall = g_ref[...]
    gc_col = _lane_pick(gall, h)
    beta_col = _lane_pick(gall, h + pl.num_programs(1))
    gc_row = gt_ref[...]

    s = state[...]
    outs = []
    for ci in range(tb // GDN_CHUNK):
        lo = ci * GDN_CHUNK
        hi = lo + GDN_CHUNK
        o, s = _gdn_chunk(qs[lo:hi, :], ks[lo:hi, :], vs[lo:hi, :], gc_col[lo:hi, :],
                          gc_row[:, lo:hi], beta_col[lo:hi, :], s)
        outs.append(o)
    state[...] = s
    o_all = jnp.concatenate(outs, axis=0)
    o_ref[...] = _rms_gate(o_all, nw_ref[...], pz_ref[...])

    @pl.when(t == pl.num_programs(2) - 1)
    def _():
        s_ref[...] = s


def _gdn_prompt(proj, gates, gates_t3, conv_w, norm_w, batch, seq, n_heads):
    tb = _pick(seq, TIME_BLOCK, GDN_CHUNK)
    nt = seq // tb
    d = n_heads * HEAD_DIM

    def rows(b, h, t):
        return b * nt + t

    def pspec(seg):
        return pl.BlockSpec((tb, HEAD_DIM), lambda b, h, t: (rows(b, h, t), seg * n_heads + h))

    def wspec(seg):
        return pl.BlockSpec((4, HEAD_DIM), lambda b, h, t: (0, seg * n_heads + h))

    return pl.pallas_call(
        _gdn_prompt_kernel,
        grid=(batch, n_heads, nt),
        in_specs=[pspec(0), pspec(1), pspec(2), pspec(3),
                  pl.BlockSpec((tb, LANES), lambda b, h, t: (rows(b, h, t), 0)),
                  pl.BlockSpec((None, 1, tb), lambda b, h, t: (h, 0, rows(b, h, t))),
                  wspec(0), wspec(1), wspec(2),
                  pl.BlockSpec((1, HEAD_DIM), lambda b, h, t: (0, 0))],
        out_specs=[pl.BlockSpec((tb, HEAD_DIM), lambda b, h, t: (rows(b, h, t), h)),
                   pl.BlockSpec((None, None, HEAD_DIM, HEAD_DIM), lambda b, h, t: (b, h, 0, 0))],
        out_shape=[jax.ShapeDtypeStruct((batch * seq, d), F32),
                   jax.ShapeDtypeStruct((batch, n_heads, HEAD_DIM, HEAD_DIM), F32)],
        scratch_shapes=[pltpu.VMEM((HEAD_DIM, HEAD_DIM), F32),
                        pltpu.VMEM((3, tb + 8, HEAD_DIM), F32),
                        pltpu.VMEM((tb, HEAD_DIM), F32),
                        pltpu.VMEM((tb, HEAD_DIM), F32),
                        pltpu.VMEM((tb, HEAD_DIM), F32)],
        compiler_params=_params(3),
        name="gdn_prompt",
    )(proj, proj, proj, proj, gates, gates_t3, conv_w, conv_w, conv_w, norm_w)


def _lower_bound(logits):
    m = jnp.max(logits, axis=0, keepdims=True)
    e = jnp.exp(logits - m)
    return e[0:1, :] / jnp.sum(e, axis=0, keepdims=True)


def _hgrn_prompt_kernel(pq_ref, pf_ref, pi_ref, pz_ref, lb_ref, nw_ref, o_ref, s_ref,
                        state_t, kbuf, bbuf, vbuf):
    t = pl.program_id(2)
    tb = pq_ref.shape[0]

    @pl.when(t == 0)
    def _():
        state_t[...] = jnp.zeros_like(state_t)

    lb = _lower_bound(lb_ref[...])
    f = lb + (1.0 - lb) * _sigmoid(pf_ref[...])
    kk = 1.0 - f
    lf = jnp.log(f)
    q = _silu(pq_ref[...]) * (HEAD_DIM ** -0.5)
    v = pi_ref[...]

    r = _iota((tb, tb), 0)
    c = _iota((tb, tb), 1)
    sub_shift = HG_SUB.bit_length() - 1
    tri = jnp.logical_and((r >> sub_shift) == (c >> sub_shift), r >= c)
    b = _dot_exact_lhs(jnp.where(tri, 1.0, 0.0).astype(BF16), lf)

    zeros = jnp.zeros((HG_SUB, HEAD_DIM), F32)
    for buf, val in ((kbuf, kk), (bbuf, b), (vbuf, v)):
        buf[0:HG_SUB, :] = zeros
        buf[HG_SUB:HG_SUB + tb, :] = val

    rmod = _iota((tb, 1), 0) & (HG_SUB - 1)
    o_intra = jnp.zeros((tb, HEAD_DIM), F32)
    for dlt in range(HG_SUB):
        k_d = kbuf[HG_SUB - dlt:HG_SUB - dlt + tb, :]
        b_d = bbuf[HG_SUB - dlt:HG_SUB - dlt + tb, :]
        v_d = vbuf[HG_SUB - dlt:HG_SUB - dlt + tb, :]
        w = jnp.sum(q * k_d * jnp.exp(b - b_d), axis=1, keepdims=True)
        o_intra = o_intra + jnp.where(rmod >= dlt, w, 0.0) * v_d

    st = state_t[...]
    outs = []
    for j in range(tb // HG_SUB):
        lo = j * HG_SUB
        hi = lo + HG_SUB
        bj = b[lo:hi, :]
        bl = bj[HG_SUB - 1:HG_SUB, :]
        qe = (q[lo:hi, :] * jnp.exp(bj)).astype(BF16)
        outs.append(o_intra[lo:hi, :] + _dot_nt(qe, st.astype(BF16)))
        ke = (kk[lo:hi, :] * jnp.exp(bl - bj)).astype(BF16)
        st = st * jnp.exp(bl) + _dot_tn(v[lo:hi, :].astype(BF16), ke)
    state_t[...] = st
    o_ref[...] = _rms_gate(jnp.concatenate(outs, axis=0), nw_ref[...], pz_ref[...])

    @pl.when(t == pl.num_programs(2) - 1)
    def _():
        s_ref[...] = st.T


def _hgrn_prompt(proj, lb_logits, norm_w, batch, seq, n_heads):
    tb = _pick(seq, TIME_BLOCK, HG_SUB)
    nt = seq // tb
    d = n_heads * HEAD_DIM
    n_lb = lb_logits.shape[0]

    def pspec(seg):
        return pl.BlockSpec((tb, HEAD_DIM), lambda b, h, t: (b * nt + t, seg * n_heads + h))

    return pl.pallas_call(
        _hgrn_prompt_kernel,
        grid=(batch, n_heads, nt),
        in_specs=[pspec(4), pspec(5), pspec(6), pspec(7),
                  pl.BlockSpec((n_lb, HEAD_DIM), lambda b, h, t: (0, h)),
                  pl.BlockSpec((1, HEAD_DIM), lambda b, h, t: (0, 0))],
        out_specs=[pl.BlockSpec((tb, HEAD_DIM), lambda b, h, t: (b * nt + t, h)),
                   pl.BlockSpec((None, None, HEAD_DIM, HEAD_DIM), lambda b, h, t: (b, h, 0, 0))],
        out_shape=[jax.ShapeDtypeStruct((batch * seq, d), F32),
                   jax.ShapeDtypeStruct((batch, n_heads, HEAD_DIM, HEAD_DIM), F32)],
        scratch_shapes=[pltpu.VMEM((HEAD_DIM, HEAD_DIM), F32),
                        pltpu.VMEM((tb + HG_SUB, HEAD_DIM), F32),
                        pltpu.VMEM((tb + HG_SUB, HEAD_DIM), F32),
                        pltpu.VMEM((tb + HG_SUB, HEAD_DIM), F32)],
        compiler_params=_params(3),
        name="hgrn_prompt",
    )(proj, proj, proj, proj, lb_logits, norm_w)


def _sample_prep_kernel(pq_ref, pk_ref, pv_ref, cq_ref, ck_ref, cv_ref, wq_ref, wk_ref, wv_ref,
                        g_ref, hq_ref, hf_ref, lb_ref,
                        q_ref, k_ref, v_ref, eg_ref, beta_ref, f_ref, qh_ref):
    h = pl.program_id(0)

    def conv(u_ref, c_ref, w_ref):
        w = w_ref[...]
        y = u_ref[...] * w[3:4, :]
        for j in range(3):
            y = y + c_ref[:, j, :] * w[j:j + 1, :]
        return _silu(y)

    q_ref[...] = _l2norm(conv(pq_ref, cq_ref, wq_ref)) * (HEAD_DIM ** -0.5)
    k_ref[...] = _l2norm(conv(pk_ref, ck_ref, wk_ref))
    v_ref[...] = conv(pv_ref, cv_ref, wv_ref)
    gall = g_ref[...]
    shape = q_ref.shape
    eg_ref[...] = jnp.broadcast_to(jnp.exp(_lane_pick(gall, h)), shape)
    beta_ref[...] = jnp.broadcast_to(_lane_pick(gall, h + pl.num_programs(0)), shape)
    lb = _lower_bound(lb_ref[...])
    f_ref[...] = lb + (1.0 - lb) * _sigmoid(hf_ref[...])
    qh_ref[...] = _silu(hq_ref[...]) * (HEAD_DIM ** -0.5)


def _sample_prep(proj, conv_state, conv_w, gates, lb_logits, n_prompt, n_sample, n_heads):
    assert n_prompt % n_sample == 0
    rb = n_prompt // n_sample
    d = n_heads * HEAD_DIM
    n_lb = lb_logits.shape[0]

    def pspec(seg):
        return pl.BlockSpec((n_sample, HEAD_DIM), lambda h: (rb, seg * n_heads + h))

    def cspec(seg):
        return pl.BlockSpec((n_sample, 3, HEAD_DIM), lambda h: (0, 0, seg * n_heads + h))

    def wspec(seg):
        return pl.BlockSpec((4, HEAD_DIM), lambda h: (0, seg * n_heads + h))

    ospec = pl.BlockSpec((n_sample, HEAD_DIM), lambda h: (0, h))
    oshape = jax.ShapeDtypeStruct((n_sample, d), F32)
    return pl.pallas_call(
        _sample_prep_kernel,
        grid=(n_heads,),
        in_specs=[pspec(0), pspec(1), pspec(2), cspec(0), cspec(1), cspec(2),
                  wspec(0), wspec(1), wspec(2),
                  pl.BlockSpec((n_sample, LANES), lambda h: (rb, 0)),
                  pspec(4), pspec(5),
                  pl.BlockSpec((n_lb, HEAD_DIM), lambda h: (0, h))],
        out_specs=[ospec] * 7,
        out_shape=[oshape] * 7,
        compiler_params=_params(1),
        name="sample_prep",
    )(proj, proj, proj, conv_state, conv_state, conv_state, conv_w, conv_w, conv_w,
      gates, proj, proj, lb_logits)


def _gdn_step_kernel(s_ref, qt_ref, kt_ref, v_ref, eg_ref, beta_ref, pz_ref, nw_ref,
                     so_ref, o_ref, obuf):
    i = pl.program_id(0)
    bt = v_ref.shape[0]
    qt = qt_ref[...]
    kt = kt_ref[...]

    def body(bb, carry):
        bg = i * bt + bb
        kcol = _lane_pick(kt, bg)
        qcol = _lane_pick(qt, bg)
        sd = s_ref[bb] * eg_ref[pl.ds(bb, 1), :]
        ks = jnp.sum(sd * kcol, axis=0, keepdims=True)
        u = beta_ref[pl.ds(bb, 1), :] * (v_ref[pl.ds(bb, 1), :] - ks)
        sn = sd + kcol * u
        so_ref[bb] = sn
        obuf[pl.ds(bb, 1), :] = jnp.sum(sn * qcol, axis=0, keepdims=True)
        return carry

    lax.fori_loop(0, bt, body, 0)
    o_ref[...] = _rms_gate(obuf[...], nw_ref[...], pz_ref[...])


def _hgrn_step_kernel(s_ref, qt_ref, ft_ref, pi_ref, pz_ref, nw_ref, so_ref, o_ref, obuf):
    i = pl.program_id(0)
    bt = pi_ref.shape[0]
    qt = qt_ref[...]
    ft = ft_ref[...]

    def body(bb, carry):
        bg = i * bt + bb
        fcol = _lane_pick(ft, bg)
        qcol = _lane_pick(qt, bg)
        sn = fcol * s_ref[bb] + (1.0 - fcol) * pi_ref[pl.ds(bb, 1), :]
        so_ref[bb] = sn
        obuf[pl.ds(bb, 1), :] = jnp.sum(sn * qcol, axis=0, keepdims=True)
        return carry

    lax.fori_loop(0, bt, body, 0)
    o_ref[...] = _rms_gate(obuf[...], nw_ref[...], pz_ref[...])


def _sample_step(kind, state, cols_t, rows, proj, norm_w, n_prompt, n_heads):
    n_sample = state.shape[0]
    bt = _pick(n_sample, 16, 8)
    rb = n_prompt // bt
    sspec = pl.BlockSpec((bt, None, HEAD_DIM, HEAD_DIM), lambda i, h: (i, h, 0, 0))
    tspec = pl.BlockSpec((HEAD_DIM, n_sample), lambda i, h: (h, 0))
    rspec = pl.BlockSpec((bt, HEAD_DIM), lambda i, h: (i, h))

    def pspec(seg):
        return pl.BlockSpec((bt, HEAD_DIM), lambda i, h: (rb + i, seg * n_heads + h))

    nspec = pl.BlockSpec((1, HEAD_DIM), lambda i, h: (0, 0))
    if kind == "gdn":
        kern = _gdn_step_kernel
        in_specs = [sspec, tspec, tspec, rspec, rspec, rspec, pspec(3), nspec]
        args = [state, *cols_t, *rows, proj, norm_w]
    else:
        kern = _hgrn_step_kernel
        in_specs = [sspec, tspec, tspec, pspec(6), pspec(7), nspec]
        args = [state, *cols_t, proj, proj, norm_w]
    return pl.pallas_call(
        kern,
        grid=(n_sample // bt, n_heads),
        in_specs=in_specs,
        out_specs=[sspec, rspec],
        out_shape=[jax.ShapeDtypeStruct(state.shape, F32),
                   jax.ShapeDtypeStruct((n_sample, n_heads * HEAD_DIM), F32)],
        scratch_shapes=[pltpu.VMEM((bt, HEAD_DIM), F32)],
        compiler_params=_params(2),
        name=kind + "_step",
    )(*args)


def _out_proj_kernel(ra_ref, rb_ref, oa_ref, oas_ref, ob_ref, obs_ref, x_ref, w_ref, g_ref, b_ref,
                     h_ref, *, alpha, n_prompt_tiles):
    is_prompt = pl.program_id(0) < n_prompt_tiles
    oa = jnp.where(is_prompt, oa_ref[...], oas_ref[...])
    ob = jnp.where(is_prompt, ob_ref[...], obs_ref[...])
    merged = _sigmoid(ra_ref[...]) * oa + _sigmoid(rb_ref[...]) * ob
    mix = _dot(merged.astype(BF16), w_ref[...])
    h_ref[...] = _layer_norm(alpha * x_ref[...] + mix, g_ref[...], b_ref[...])


def _out_projection(proj, oa, oa_s, ob, ob_s, x, w_out, ln_g, ln_b, alpha):
    n, d = x.shape
    n_prompt = oa.shape[0]
    tm = oa_s.shape[0]
    assert n_prompt % tm == 0 and n == n_prompt + tm
    npt = n_prompt // tm
    row = pl.BlockSpec((tm, d), lambda i: (i, 0))
    prow = pl.BlockSpec((tm, d), lambda i: (jnp.minimum(i, npt - 1), 0))
    srow = pl.BlockSpec((tm, d), lambda i: (0, 0))
    vec = pl.BlockSpec((1, d), lambda i: (0, 0))
    return pl.pallas_call(
        functools.partial(_out_proj_kernel, alpha=alpha, n_prompt_tiles=npt),
        grid=(n // tm,),
        in_specs=[pl.BlockSpec((tm, d), lambda i: (i, 8)),
                  pl.BlockSpec((tm, d), lambda i: (i, 9)),
                  prow, srow, prow, srow, row,
                  pl.BlockSpec((d, d), lambda i: (0, 0)), vec, vec],
        out_specs=row,
        out_shape=jax.ShapeDtypeStruct((n, d), F32),
        compiler_params=_params(1),
        name="out_projection",
    )(proj, proj, oa, oa_s, ob, ob_s, x, w_out, ln_g, ln_b)


def _router_kernel(h_ref, w_ref, b_ref, r_ref, cnt_ref, carry, *, n_experts):
    i = pl.program_id(0)

    @pl.when(i == 0)
    def _():
        carry[...] = jnp.zeros_like(carry)

    logits = _dot3(h_ref[...], w_ref[...]) + b_ref[...]
    tm = logits.shape[0]
    lane = _iota(logits.shape, 1)
    x = jnp.where(lane < n_experts, logits, -jnp.inf)
    vals, idxs = [], []
    for _ in range(TOP_K):
        m = jnp.max(x, axis=1, keepdims=True)
        idx = jnp.min(jnp.where(x == m, lane, LANES), axis=1, keepdims=True)
        vals.append(m)
        idxs.append(idx)
        x = jnp.where(lane == idx, -jnp.inf, x)
    es = [jnp.exp(v - vals[0]) for v in vals]
    den = es[0] + es[1] + es[2] + es[3]
    hot = jnp.zeros(logits.shape, F32)
    for idx in idxs:
        hot = hot + jnp.where(lane == idx, 1.0, 0.0)
    r = _iota((tm, tm), 0)
    c = _iota((tm, tm), 1)
    before = _dot(jnp.where(r > c, 1.0, 0.0).astype(BF16), hot.astype(BF16)) + carry[...]
    out = jnp.zeros(logits.shape, F32)
    for k in range(TOP_K):
        rank = jnp.sum(jnp.where(lane == idxs[k], before, 0.0), axis=1, keepdims=True)
        out = out + jnp.where(lane == k, idxs[k].astype(F32), 0.0)
        out = out + jnp.where(lane == TOP_K + k, rank, 0.0)
        out = out + jnp.where(lane == 2 * TOP_K + k, es[k] / den, 0.0)
    r_ref[...] = out
    carry[...] = carry[...] + jnp.sum(hot, axis=0, keepdims=True)
    cnt_ref[...] = carry[...]


def _router(h, w_r, b_r, n_experts):
    n, d = h.shape
    tm = _pick(n, 208, 16)
    return pl.pallas_call(
        functools.partial(_router_kernel, n_experts=n_experts),
        grid=(n // tm,),
        in_specs=[pl.BlockSpec((tm, d), lambda i: (i, 0)),
                  pl.BlockSpec((d, LANES), lambda i: (0, 0)),
                  pl.BlockSpec((1, LANES), lambda i: (0, 0))],
        out_specs=[pl.BlockSpec((tm, LANES), lambda i: (i, 0)),
                   pl.BlockSpec((1, LANES), lambda i: (0, 0))],
        out_shape=[jax.ShapeDtypeStruct((n, LANES), F32),
                   jax.ShapeDtypeStruct((1, LANES), F32)],
        scratch_shapes=[pltpu.VMEM((1, LANES), F32)],
        compiler_params=_params(1),
        name="router",
    )(h, w_r, b_r)


def _dispatch_kernel(dest_ref, h_ref, xin_ref, xb_ref, packed, sem):
    del xin_ref
    i = pl.program_id(0)
    tm, d = h_ref.shape
    half = d // 2
    bits = pltpu.bitcast(h_ref[...].astype(BF16).astype(F32), U32)
    packed[...] = (bits[:, :half] >> 16) | (bits[:, half:] & jnp.uint32(0xFFFF0000))

    def row_copy(r, k):
        dst = dest_ref[(i * tm + r) * TOP_K + k]
        return pltpu.make_async_copy(packed.at[pl.ds(r, 1), :], xb_ref.at[pl.ds(dst, 1), :], sem)

    def start(r, carry):
        for k in range(TOP_K):
            row_copy(r, k).start()
        return carry

    def wait(r, carry):
        for k in range(TOP_K):
            row_copy(r, k).wait()
        return carry

    lax.fori_loop(0, tm, start, 0)
    lax.fori_loop(0, tm, wait, 0)


def _dispatch(dest_flat, h, xb_init):
    n, d = h.shape
    tm = _pick(n, 128, 8)
    return pl.pallas_call(
        _dispatch_kernel,
        grid_spec=pltpu.PrefetchScalarGridSpec(
            num_scalar_prefetch=1,
            grid=(n // tm,),
            in_specs=[pl.BlockSpec((tm, d), lambda i, dest: (i, 0)),
                      pl.BlockSpec(memory_space=pl.ANY)],
            out_specs=pl.BlockSpec(memory_space=pl.ANY),
            scratch_shapes=[pltpu.VMEM((tm, d // 2), U32),
                            pltpu.SemaphoreType.DMA(())]),
        out_shape=jax.ShapeDtypeStruct(xb_init.shape, U32),
        input_output_aliases={2: 0},
        compiler_params=_params(1),
        name="dispatch",
    )(dest_flat, h, xb_init)


def _deinterleave_matrix(n):
    r = _iota((n, n), 0)
    c = _iota((n, n), 1)
    src = jnp.where(c < n // 2, 2 * c, 2 * (c - n // 2) + 1)
    return jnp.where(r == src, 1.0, 0.0).astype(BF16)


def _moe_kernel(tile_ref, exp_ref, blk0_ref, nblk_ref, zero_ref, x_ref, wgu_ref, bgu_ref, wdn_ref,
                bdn_ref, y_ref, wgu_b, wdn_b):
    del tile_ref, exp_ref
    it = pl.program_id(0)
    j = pl.program_id(1)
    blk0 = blk0_ref[it]
    nblk = nblk_ref[it]
    half = x_ref.shape[1]
    sub = MOE_COL_TILE // 2

    @pl.when(jnp.logical_and(j == 0, zero_ref[it] == 1))
    def _():
        y_ref[...] = jnp.zeros_like(y_ref)

    @pl.when(nblk > 0)
    def _():
        wgu_b[...] = wgu_ref[...].astype(BF16)
        wdn_b[...] = wdn_ref[...].astype(BF16)
        perm = _deinterleave_matrix(sub)
        bgu = bgu_ref[...]
        bdn = bdn_ref[...]

        def body(bi, carry):
            r0 = pl.multiple_of((blk0 + bi) * MOE_ROW_BLOCK, MOE_ROW_BLOCK)
            words = x_ref[pl.ds(r0, MOE_ROW_BLOCK), :]
            x_lo = pltpu.bitcast(words << 16, F32).astype(BF16)
            x_hi = pltpu.bitcast(words & jnp.uint32(0xFFFF0000), F32).astype(BF16)
            hfull = _dot(x_lo, wgu_b[0:half, :]) + _dot(x_hi, wgu_b[half:2 * half, :]) + bgu
            acts = []
            for s in range(MOE_COL_TILE // sub):
                hs = hfull[:, s * sub:(s + 1) * sub]
                hi, lo = _hi_lo(hs)
                both = _dot(jnp.concatenate([hi, lo], axis=0), perm)
                gu = both[0:MOE_ROW_BLOCK, :] + both[MOE_ROW_BLOCK:2 * MOE_ROW_BLOCK, :]
                gate = jnp.minimum(gu[:, 0:sub // 2], SWIGLU_LIMIT)
                up = jnp.clip(gu[:, sub // 2:sub], -SWIGLU_LIMIT, SWIGLU_LIMIT)
                acts.append((up + 1.0) * (gate * _sigmoid(gate * SWIGLU_ALPHA)))
            act = jnp.concatenate(acts, axis=1).astype(BF16)
            contrib = _dot(act, wdn_b[...])
            prev = jnp.where(j == 0, jnp.broadcast_to(bdn, contrib.shape),
                             y_ref[pl.ds(r0, MOE_ROW_BLOCK), :])
            y_ref[pl.ds(r0, MOE_ROW_BLOCK), :] = prev + contrib
            return carry

        lax.fori_loop(0, nblk, body, 0)


def _moe_gemm(item_tile, item_exp, item_blk0, item_nblk, item_zero, xb, w_gu, b_gu, w_dn, b_dn):
    p_rows, half = xb.shape
    n_exp, d, two_de = w_gu.shape
    de = two_de // 2
    n_items = item_tile.shape[0]
    nj = two_de // MOE_COL_TILE
    dn_rows = MOE_COL_TILE // 2

    def live_j(it, j, nblk):
        return jnp.where(nblk[it] > 0, j, nj - 1)

    return pl.pallas_call(
        _moe_kernel,
        grid_spec=pltpu.PrefetchScalarGridSpec(
            num_scalar_prefetch=5,
            grid=(n_items, nj),
            in_specs=[
                pl.BlockSpec((MOE_ROW_TILE, half), lambda it, j, tl, ex, b0, nb, zf: (tl[it], 0)),
                pl.BlockSpec((None, d, MOE_COL_TILE),
                             lambda it, j, tl, ex, b0, nb, zf: (ex[it], 0, live_j(it, j, nb))),
                pl.BlockSpec((None, 1, MOE_COL_TILE),
                             lambda it, j, tl, ex, b0, nb, zf: (ex[it], 0, live_j(it, j, nb))),
                pl.BlockSpec((None, dn_rows, d),
                             lambda it, j, tl, ex, b0, nb, zf: (ex[it], live_j(it, j, nb), 0)),
                pl.BlockSpec((None, 1, d), lambda it, j, tl, ex, b0, nb, zf: (ex[it], 0, 0)),
            ],
            out_specs=pl.BlockSpec((MOE_ROW_TILE, d), lambda it, j, tl, ex, b0, nb, zf: (tl[it], 0)),
            scratch_shapes=[pltpu.VMEM((d, MOE_COL_TILE), BF16),
                            pltpu.VMEM((dn_rows, d), BF16)]),
        out_shape=jax.ShapeDtypeStruct((p_rows, d), F32),
        compiler_params=_params(2),
        name="moe_gemm",
    )(item_tile, item_exp, item_blk0, item_nblk, item_zero, xb, w_gu, b_gu.reshape(n_exp, 1, two_de),
      w_dn, b_dn.reshape(n_exp, 1, d))


def _combine_kernel(dest_ref, h_ref, r_ref, g_ref, b_ref, yb_ref, o_ref, rows, sem, *, alpha):
    i = pl.program_id(0)
    tm, d = h_ref.shape

    def row_copy(r, k):
        src = dest_ref[(i * tm + r) * TOP_K + k]
        return pltpu.make_async_copy(yb_ref.at[pl.ds(src, 1), :], rows.at[k, pl.ds(r, 1), :], sem)

    def start(r, carry):
        for k in range(TOP_K):
            row_copy(r, k).start()
        return carry

    def wait(r, carry):
        for k in range(TOP_K):
            row_copy(r, k).wait()
        return carry

    lax.fori_loop(0, tm, start, 0)
    lax.fori_loop(0, tm, wait, 0)
    rr = r_ref[...]
    ffn = jnp.zeros((tm, d), F32)
    for k in range(TOP_K):
        ffn = ffn + _lane_pick(rr, 2 * TOP_K + k) * rows[k]
    o_ref[...] = _layer_norm(alpha * h_ref[...] + ffn, g_ref[...], b_ref[...])


def _combine(dest_flat, h, route, ln_g, ln_b, yb, alpha):
    n, d = h.shape
    tm = _pick(n, 128, 8)
    return pl.pallas_call(
        functools.partial(_combine_kernel, alpha=alpha),
        grid_spec=pltpu.PrefetchScalarGridSpec(
            num_scalar_prefetch=1,
            grid=(n // tm,),
            in_specs=[pl.BlockSpec((tm, d), lambda i, dest: (i, 0)),
                      pl.BlockSpec((tm, LANES), lambda i, dest: (i, 0)),
                      pl.BlockSpec((1, d), lambda i, dest: (0, 0)),
                      pl.BlockSpec((1, d), lambda i, dest: (0, 0)),
                      pl.BlockSpec(memory_space=pl.ANY)],
            out_specs=pl.BlockSpec((tm, d), lambda i, dest: (i, 0)),
            scratch_shapes=[pltpu.VMEM((TOP_K, tm, d), F32),
                            pltpu.SemaphoreType.DMA(())]),
        out_shape=jax.ShapeDtypeStruct((n, d), F32),
        compiler_params=_params(1),
        name="combine",
    )(dest_flat, h, route, ln_g, ln_b, yb)


def _routing_tables(route, counts_row, n_experts):
    n = route.shape[0]
    e_idx = route[:, 0:TOP_K].astype(I32)
    rank = route[:, TOP_K:2 * TOP_K].astype(I32)
    counts = counts_row[0, :n_experts].astype(I32)
    pcounts = (counts + MOE_ROW_BLOCK - 1) // MOE_ROW_BLOCK * MOE_ROW_BLOCK
    pends = jnp.cumsum(pcounts)
    pstarts = pends - pcounts
    dest = (pstarts[e_idx] + rank).reshape(-1)

    p_rows = -(-(n * TOP_K + n_experts * MOE_ROW_BLOCK) // MOE_ROW_TILE) * MOE_ROW_TILE
    n_blocks = p_rows // MOE_ROW_BLOCK
    bpt = MOE_ROW_TILE // MOE_ROW_BLOCK
    max_items = p_rows // MOE_ROW_TILE + n_experts
    blk = jnp.arange(n_blocks, dtype=I32)
    valid = blk * MOE_ROW_BLOCK < pends[-1]
    blk_e = jnp.minimum(jnp.searchsorted(pends, blk * MOE_ROW_BLOCK, side="right"),
                        n_experts - 1).astype(I32)
    first = valid & ((blk % bpt == 0) | (blk_e != jnp.roll(blk_e, 1)))
    item_of_blk = jnp.cumsum(first.astype(I32)) - 1
    n_items = jnp.sum(first.astype(I32))
    item_first = jnp.zeros((max_items,), I32).at[jnp.where(first, item_of_blk, max_items)].set(
        blk, mode="drop")
    item_nblk = jnp.zeros((max_items,), I32).at[jnp.where(valid, item_of_blk, max_items)].add(
        1, mode="drop")
    it = jnp.arange(max_items, dtype=I32)
    live = it < n_items
    item_first = item_first[jnp.minimum(it, n_items - 1)]
    item_nblk = jnp.where(live, item_nblk, 0)
    item_blk0 = item_first % bpt
    n_tiles = p_rows // MOE_ROW_TILE
    last_tile = item_first[max_items - 1] // bpt
    idle_tile = last_tile + 1 + (it - n_items)
    item_tile = jnp.where(live, item_first // bpt, jnp.minimum(idle_tile, n_tiles - 1))
    item_zero = jnp.where(live, (item_blk0 == 0) & (item_nblk > 0), idle_tile < n_tiles)
    return (dest, p_rows, item_tile, blk_e[item_first], item_blk0, item_nblk,
            item_zero.astype(I32))


def kernel(x_prompt, x_sample, state_gdn_conv, state_gdn_s, state_hgrn_s, w_in, gdn_conv_w,
           gdn_a_log, gdn_dt_bias, gdn_norm_w, hg_lb_logits, hg_norm_w, w_out, ln1_g, ln1_b,
           w_router, b_router, w_gate_up, b_gate_up, w_down, b_down, ln2_g, ln2_b):
    depth = w_in.shape[0]
    assert depth == 1
    batch, seq, d = x_prompt.shape
    n_sample = x_sample.shape[0]
    assert x_sample.shape[1] == 1
    n_heads = d // HEAD_DIM
    n_prompt = batch * seq
    n_rows = n_prompt + n_sample
    n_experts = w_router.shape[-1]
    alpha = (2.0 * depth) ** 0.25
    qkv = 3 * d
    ab0 = qkv
    ab1 = qkv + 2 * n_heads

    x_all = jnp.concatenate([x_prompt.reshape(n_prompt, d), x_sample.reshape(n_sample, d)],
                            axis=0).astype(F32)
    w = w_in[0]
    w_main = jnp.concatenate([w[:, :ab0], w[:, ab1:]], axis=1).astype(BF16)
    w_ab = jnp.pad(w[:, ab0:ab1].astype(F32), ((0, 0), (0, LANES - 2 * n_heads)))
    prm = jnp.zeros((8, LANES), F32)
    prm = prm.at[0, :n_heads].set(gdn_a_log[0].astype(F32))
    prm = prm.at[1, :n_heads].set(gdn_dt_bias[0].astype(F32))
    conv_w = gdn_conv_w[0].astype(F32)
    gdn_nw = gdn_norm_w[0].astype(F32).reshape(1, HEAD_DIM)
    hg_nw = hg_norm_w[0].astype(F32).reshape(1, HEAD_DIM)
    lb_logits = hg_lb_logits.astype(F32)

    proj = _in_projection(x_all, w_main)
    gates, gates_t = _gates(x_all, w_ab, prm, n_prompt, n_heads)
    gates_t3 = gates_t[:n_heads].reshape(n_heads, 1, n_rows)

    oa, gdn_s_prompt = _gdn_prompt(proj, gates, gates_t3, conv_w, gdn_nw, batch, seq, n_heads)
    ob, hg_s_prompt = _hgrn_prompt(proj, lb_logits, hg_nw, batch, seq, n_heads)

    conv_state = state_gdn_conv[0].astype(F32)
    sq, sk, sv, seg, sbeta, sf, sqh = _sample_prep(proj, conv_state, conv_w, gates, lb_logits,
                                                   n_prompt, n_sample, n_heads)
    gdn_s_sample, oa_s = _sample_step("gdn", state_gdn_s[0].astype(F32), (sq.T, sk.T),
                                      (sv, seg, sbeta), proj, gdn_nw, n_prompt, n_heads)
    hg_s_sample, ob_s = _sample_step("hgrn", state_hgrn_s[0].astype(F32), (sqh.T, sf.T), (),
                                     proj, hg_nw, n_prompt, n_heads)

    h = _out_projection(proj, oa, oa_s, ob, ob_s, x_all, w_out[0].astype(BF16),
                        ln1_g[0].astype(F32).reshape(1, d), ln1_b[0].astype(F32).reshape(1, d),
                        alpha)

    w_r = jnp.pad(w_router[0].astype(F32), ((0, 0), (0, LANES - n_experts)))
    b_r = jnp.pad(b_router[0].astype(F32), (0, LANES - n_experts)).reshape(1, LANES)
    route, counts_row = _router(h, w_r, b_r, n_experts)
    dest, p_rows, item_tile, item_exp, item_blk0, item_nblk, item_zero = _routing_tables(
        route, counts_row, n_experts)

    xb = _dispatch(dest, h, jnp.zeros((p_rows, d // 2), U32))
    yb = _moe_gemm(item_tile, item_exp, item_blk0, item_nblk, item_zero, xb, w_gate_up[0],
                   b_gate_up[0], w_down[0], b_down[0])
    y = _combine(dest, h, route, ln2_g[0].astype(F32).reshape(1, d),
                 ln2_b[0].astype(F32).reshape(1, d), yb, alpha)

    y_prompt = y[:n_prompt].reshape(batch, seq, d).astype(x_prompt.dtype)
    y_sample = y[n_prompt:].reshape(n_sample, 1, d).astype(x_sample.dtype)
    new_conv_p = proj[:n_prompt, :qkv].reshape(batch, seq, qkv)[:, seq - 3:, :][None]
    new_conv_s = jnp.concatenate([conv_state[:, 1:, :], proj[n_prompt:, None, :qkv]], axis=1)[None]
    sdt = state_gdn_s.dtype
    return (y_prompt, y_sample,
            new_conv_p.astype(state_gdn_conv.dtype), gdn_s_prompt[None].astype(sdt),
            hg_s_prompt[None].astype(state_hgrn_s.dtype),
            new_conv_s.astype(state_gdn_conv.dtype), gdn_s_sample[None].astype(sdt),
            hg_s_sample[None].astype(state_hgrn_s.dtype))
```

```python
import functools

import jax
import jax.numpy as jnp
from jax import lax
from jax.experimental import pallas as pl
from jax.experimental.pallas import tpu as pltpu

F32 = jnp.float32
BF16 = jnp.bfloat16
I32 = jnp.int32
U32 = jnp.uint32

HEAD_DIM = 128
LANES = 128
GDN_CHUNK = 64
HG_SUB = 16
TIME_BLOCK = 256
GDN_HEADS_PER_STEP = 4
HG_HEADS_PER_STEP = 2
TOP_K = 4
MOE_ROW_BLOCK = 256
MOE_ROW_TILE = 1024
MOE_COL_TILE = 512
SWIGLU_LIMIT = 7.0
SWIGLU_ALPHA = 1.702
LN_EPS = 1e-5
RMS_EPS = 1e-6
L2_EPS = 1e-6
VMEM_LIMIT = 56 * 1024 * 1024


def _params(n_axes, vmem=VMEM_LIMIT):
    return pltpu.CompilerParams(dimension_semantics=("arbitrary",) * n_axes,
                                vmem_limit_bytes=vmem)


def _pick(n, target, mult):
    best = None
    for d in range(mult, min(n, target) + 1, mult):
        if n % d == 0:
            best = d
    assert best is not None, (n, target, mult)
    return best


def _dot(a, b):
    return jnp.dot(a, b, preferred_element_type=F32)


def _dot_nt(a, b):
    return lax.dot_general(a, b, (((1,), (1,)), ((), ())), preferred_element_type=F32)


def _dot_tn(a, b):
    return lax.dot_general(a, b, (((0,), (0,)), ((), ())), preferred_element_type=F32)


def _hi_lo(x):
    hi = x.astype(BF16)
    lo = (x - hi.astype(F32)).astype(BF16)
    return hi, lo


def _dot3(a, b):
    ah, al = _hi_lo(a)
    bh, bl = _hi_lo(b)
    return _dot(ah, bh) + (_dot(ah, bl) + _dot(al, bh))


def _dot_exact_lhs(m_bf16, x):
    p1 = x.astype(BF16)
    r1 = x - p1.astype(F32)
    p2 = r1.astype(BF16)
    p3 = (r1 - p2.astype(F32)).astype(BF16)
    return _dot(m_bf16, p1) + (_dot(m_bf16, p2) + _dot(m_bf16, p3))


def _sigmoid(x):
    return 1.0 / (1.0 + jnp.exp(-x))


def _silu(x):
    return x * _sigmoid(x)


def _softplus(x):
    return jnp.maximum(x, 0.0) + jnp.log(1.0 + jnp.exp(-jnp.abs(x)))


def _iota(shape, dim):
    return lax.broadcasted_iota(I32, shape, dim)


def _layer_norm(y, g, b):
    mu = jnp.mean(y, axis=-1, keepdims=True)
    yc = y - mu
    var = jnp.mean(yc * yc, axis=-1, keepdims=True)
    return yc * lax.rsqrt(var + LN_EPS) * g + b


def _rms_gate(o, w, z):
    return o * lax.rsqrt(jnp.mean(o * o, axis=-1, keepdims=True) + RMS_EPS) * w * _silu(z)


def _l2norm(x):
    return x * lax.rsqrt(jnp.sum(x * x, axis=-1, keepdims=True) + L2_EPS)


def _lane_pick(x, idx):
    lane = _iota(x.shape, 1)
    return jnp.sum(jnp.where(lane == idx, x, 0.0), axis=1, keepdims=True)


def _mm_kernel(x_ref, w_ref, o_ref):
    o_ref[...] = _dot(x_ref[...].astype(BF16), w_ref[...].astype(BF16))


def _in_projection(x, w, no, tn_target):
    n, d = x.shape
    tm = _pick(n, 1040, 16)
    tn = _pick(no, tn_target, LANES)
    return pl.pallas_call(
        _mm_kernel,
        grid=(n // tm, no // tn),
        in_specs=[pl.BlockSpec((tm, d), lambda i, j: (i, 0)),
                  pl.BlockSpec((d, tn), lambda i, j: (0, j))],
        out_specs=pl.BlockSpec((tm, tn), lambda i, j: (i, j)),
        out_shape=jax.ShapeDtypeStruct((n, no), F32),
        compiler_params=_params(2),
        name="in_projection",
    )(x, w)


def _gates_kernel(x_ref, w_ref, prm_ref, g_ref, gt_ref, *, n_prompt_tiles, n_heads):
    i = pl.program_id(0)
    ab = _dot3(x_ref[...], w_ref[...])
    prm = prm_ref[...]
    g = -jnp.exp(prm[0:1]) * _softplus(ab + prm[1:2])
    beta = _sigmoid(ab)
    tm = ab.shape[0]
    r = _iota((tm, tm), 0)
    c = _iota((tm, tm), 1)
    shift = jnp.where(i < n_prompt_tiles, GDN_CHUNK.bit_length() - 1, 0)
    tri = jnp.logical_and((r >> shift) == (c >> shift), r >= c)
    gc = _dot_exact_lhs(jnp.where(tri, 1.0, 0.0).astype(BF16), g)
    lane = _iota(ab.shape, 1)
    out = jnp.where(lane < n_heads, gc, beta)
    g_ref[...] = out
    gt_ref[...] = out.T


def _gates(x, w_ab, prm, n_prompt, n_heads):
    n, d = x.shape
    tm = LANES
    assert n % tm == 0 and n_prompt % tm == 0 and tm % GDN_CHUNK == 0
    kern = functools.partial(_gates_kernel, n_prompt_tiles=n_prompt // tm, n_heads=n_heads)
    return pl.pallas_call(
        kern,
        grid=(n // tm,),
        in_specs=[pl.BlockSpec((tm, d), lambda i: (i, 0)),
                  pl.BlockSpec((d, LANES), lambda i: (0, 0)),
                  pl.BlockSpec((8, LANES), lambda i: (0, 0))],
        out_specs=[pl.BlockSpec((tm, LANES), lambda i: (i, 0)),
                   pl.BlockSpec((LANES, tm), lambda i: (0, i))],
        out_shape=[jax.ShapeDtypeStruct((n, LANES), F32),
                   jax.ShapeDtypeStruct((LANES, n), F32)],
        compiler_params=_params(1),
        name="gates",
    )(x, w_ab, prm)


def _split_dot3(ah, al, bh, bl):
    return _dot(ah, bh) + (_dot(ah, bl) + _dot(al, bh))


def _unit_lower_inverses(mats):
    c = mats[0].shape[0]
    r = _iota((c, c), 0)
    col = _iota((c, c), 1)
    eye = jnp.where(r == col, 1.0, 0.0)
    pair = jnp.logical_and((r >> 1) == (col >> 1), r > col)
    invs = [eye - jnp.where(pair, a, 0.0) for a in mats]
    level = 2
    while (1 << level) <= c:
        half = level - 1
        mask = jnp.logical_and(
            (r >> level) == (col >> level),
            jnp.logical_and(((r >> half) & 1) == 1, ((col >> half) & 1) == 0))
        lows = [_hi_lo(jnp.where(mask, a, 0.0)) for a in mats]
        inv_s = [_hi_lo(inv) for inv in invs]
        xs = [_split_dot3(lh, ll, ih, il) for (lh, ll), (ih, il) in zip(lows, inv_s)]
        x_s = [_hi_lo(x) for x in xs]
        invs = [inv - _split_dot3(ih, il, xh, xl)
                for inv, (ih, il), (xh, xl) in zip(invs, inv_s, x_s)]
        level += 1
    return invs


def _gdn_prompt_kernel(pq_ref, pk_ref, pv_ref, pz_ref, g_ref, gt_ref, wq_ref, wk_ref, wv_ref,
                       nw_ref, o_ref, s_ref, state, cbuf, qs, ks, vs):
    hg = pl.program_id(1)
    t = pl.program_id(2)
    tb = pq_ref.shape[0]
    hb = state.shape[0]
    n_heads = pl.num_programs(1) * hb
    cl = GDN_CHUNK

    @pl.when(t == 0)
    def _():
        state[...] = jnp.zeros_like(state)
        cbuf[:, 0:8, :] = jnp.zeros((3, 8, hb * HEAD_DIM), F32)

    def conv(idx, u_ref, w_ref):
        u = u_ref[...]
        cbuf[idx, 8:8 + tb, :] = u
        w = w_ref[...]
        y = cbuf[idx, 5:5 + tb, :] * w[0:1, :]
        for j in range(1, 4):
            y = y + cbuf[idx, 5 + j:5 + j + tb, :] * w[j:j + 1, :]
        cbuf[idx, 0:8, :] = u[tb - 8:tb, :]
        return _silu(y)

    qc = conv(0, pq_ref, wq_ref)
    kc = conv(1, pk_ref, wk_ref)
    vs[...] = conv(2, pv_ref, wv_ref)
    for i in range(hb):
        sl = slice(i * HEAD_DIM, (i + 1) * HEAD_DIM)
        qs[:, sl] = _l2norm(qc[:, sl]) * (HEAD_DIM ** -0.5)
        ks[:, sl] = _l2norm(kc[:, sl])

    gall = g_ref[...]
    gt = gt_ref[...]
    gc_cols = [_lane_pick(gall, hg * hb + i) for i in range(hb)]
    beta_cols = [_lane_pick(gall, n_heads + hg * hb + i) for i in range(hb)]

    r = _iota((cl, cl), 0)
    col = _iota((cl, cl), 1)
    incl = r >= col
    strict = r > col

    pairs = [(c, i) for c in range(tb // cl) for i in range(hb)]
    pre = []
    for c, i in pairs:
        rows = slice(c * cl, (c + 1) * cl)
        sl = slice(i * HEAD_DIM, (i + 1) * HEAD_DIM)
        q = qs[rows, sl]
        k = ks[rows, sl]
        v = vs[rows, sl]
        gcc = gc_cols[i][rows, :]
        gcr = gt[i:i + 1, rows]
        bc = beta_cols[i][rows, :]
        decay = jnp.where(incl, jnp.exp(jnp.where(incl, gcc - gcr, 0.0)), 0.0)
        kb = k.astype(BF16)
        a = jnp.where(strict, bc * decay * _dot_nt(kb, kb), 0.0)
        aqk = (_dot_nt(q.astype(BF16), kb) * decay).astype(BF16)
        egc = jnp.exp(gcc)
        g_last = gcc[cl - 1:cl, :]
        pre.append(dict(a=a, aqk=aqk, rhs=_hi_lo(jnp.concatenate([bc * egc * k, bc * v], axis=1)),
                        qd=(q * egc).astype(BF16), ke=(k * jnp.exp(g_last - gcc)).astype(BF16),
                        g_end=jnp.exp(g_last)))
    invs = _unit_lower_inverses([p["a"] for p in pre])
    for p, inv in zip(pre, invs):
        ih, il = _hi_lo(inv)
        wu = _split_dot3(ih, il, *p["rhs"])
        p["w"] = wu[:, :HEAD_DIM].astype(BF16)
        p["u0"] = wu[:, HEAD_DIM:]

    s = [state[i] for i in range(hb)]
    outs = [[] for _ in range(hb)]
    for c in range(tb // cl):
        ps = [pre[c * hb + i] for i in range(hb)]
        sb = [x.astype(BF16) for x in s]
        ws = [_dot(p["w"], b) for p, b in zip(ps, sb)]
        qsd = [_dot(p["qd"], b) for p, b in zip(ps, sb)]
        ub = [(p["u0"] - x).astype(BF16) for p, x in zip(ps, ws)]
        s = [p["g_end"] * x + _dot_tn(p["ke"], u) for p, x, u in zip(ps, s, ub)]
        for i in range(hb):
            outs[i].append(qsd[i] + _dot(ps[i]["aqk"], ub[i]))
    nw = nw_ref[...]
    for i in range(hb):
        sl = slice(i * HEAD_DIM, (i + 1) * HEAD_DIM)
        state[i] = s[i]
        o_ref[:, sl] = _rms_gate(jnp.concatenate(outs[i], axis=0), nw, pz_ref[:, sl])

    @pl.when(t == pl.num_programs(2) - 1)
    def _():
        for i in range(hb):
            s_ref[i] = s[i]


def _gdn_prompt(proj, projb, gates, gates_t, conv_w, norm_w, batch, seq, n_heads):
    tb = _pick(seq, TIME_BLOCK, GDN_CHUNK)
    nt = seq // tb
    hb = _pick(n_heads, GDN_HEADS_PER_STEP, 1)
    ng = n_heads // hb
    d = n_heads * HEAD_DIM
    n_rows = gates_t.shape[1]
    gates_t3 = gates_t[:n_heads].reshape(ng, hb, n_rows)

    def rows(b, h, t):
        return b * nt + t

    def pspec(seg):
        return pl.BlockSpec((tb, hb * HEAD_DIM), lambda b, h, t: (rows(b, h, t), seg * ng + h))

    def wspec(seg):
        return pl.BlockSpec((4, hb * HEAD_DIM), lambda b, h, t: (0, seg * ng + h))

    return pl.pallas_call(
        _gdn_prompt_kernel,
        grid=(batch, ng, nt),
        in_specs=[pspec(0), pspec(1), pspec(2), pspec(0),
                  pl.BlockSpec((tb, LANES), lambda b, h, t: (rows(b, h, t), 0)),
                  pl.BlockSpec((None, hb, tb), lambda b, h, t: (h, 0, rows(b, h, t))),
                  wspec(0), wspec(1), wspec(2),
                  pl.BlockSpec((1, HEAD_DIM), lambda b, h, t: (0, 0))],
        out_specs=[pl.BlockSpec((tb, hb * HEAD_DIM), lambda b, h, t: (rows(b, h, t), h)),
                   pl.BlockSpec((None, hb, HEAD_DIM, HEAD_DIM), lambda b, h, t: (b, h, 0, 0))],
        out_shape=[jax.ShapeDtypeStruct((batch * seq, d), F32),
                   jax.ShapeDtypeStruct((batch, n_heads, HEAD_DIM, HEAD_DIM), F32)],
        scratch_shapes=[pltpu.VMEM((hb, HEAD_DIM, HEAD_DIM), F32),
                        pltpu.VMEM((3, tb + 8, hb * HEAD_DIM), F32),
                        pltpu.VMEM((tb, hb * HEAD_DIM), F32),
                        pltpu.VMEM((tb, hb * HEAD_DIM), F32),
                        pltpu.VMEM((tb, hb * HEAD_DIM), F32)],
        compiler_params=_params(3),
        name="gdn_prompt",
    )(proj, proj, proj, projb, gates, gates_t3, conv_w, conv_w, conv_w, norm_w)


def _lower_bound(logits):
    m = jnp.max(logits, axis=0, keepdims=True)
    e = jnp.exp(logits - m)
    return e[0:1, :] / jnp.sum(e, axis=0, keepdims=True)


def _hgrn_prompt_kernel(pq_ref, pf_ref, pi_ref, pz_ref, lb_ref, nw_ref, o_ref, s_ref,
                        state_t):
    t = pl.program_id(2)
    tb = pq_ref.shape[0]
    hb = state_t.shape[0]

    @pl.when(t == 0)
    def _():
        state_t[...] = jnp.zeros_like(state_t)

    lb = _lower_bound(lb_ref[...])
    f = lb + (1.0 - lb) * _sigmoid(pf_ref[...])
    kk = 1.0 - f
    lf = jnp.log2(f)
    q = _silu(pq_ref[...]) * (HEAD_DIM ** -0.5)
    v = pi_ref[...]

    r = _iota((tb, tb), 0)
    c = _iota((tb, tb), 1)
    sub_shift = HG_SUB.bit_length() - 1
    tri = jnp.logical_and((r >> sub_shift) == (c >> sub_shift), r >= c)
    b = _dot_exact_lhs(jnp.where(tri, 1.0, 0.0).astype(BF16), lf)

    width = hb * HEAD_DIM
    heads = [slice(i * HEAD_DIM, (i + 1) * HEAD_DIM) for i in range(hb)]

    half = HG_SUB // 2
    n_sub = tb // HG_SUB

    def halves(x):
        x4 = x.reshape(n_sub, 2, half, width)
        return x4[:, 0], x4[:, 1]

    q_lo, q_hi = halves(q)
    b_lo, b_hi = halves(b)
    k_lo, k_hi = halves(kk)
    v_lo, v_hi = halves(v)
    row = _iota((1, half, 1), 1)

    def rot(x, d):
        return x if d == 0 else pltpu.roll(x, d, 1)

    def add_terms(acc, qx, bx, kp, bp, vp, ok):
        prod = qx * kp * jnp.exp2(bx - bp)
        for i, sl in enumerate(heads):
            w = jnp.sum(prod[:, :, sl], axis=2, keepdims=True)
            if ok is not None:
                w = jnp.where(ok, w, 0.0)
            acc[i] = acc[i] + w * vp[:, :, sl]

    acc_lo = [jnp.zeros((n_sub, half, HEAD_DIM), F32) for _ in range(hb)]
    acc_hi = [jnp.zeros((n_sub, half, HEAD_DIM), F32) for _ in range(hb)]
    for d in range(half):
        ok = None if d == 0 else row >= d
        kl, bl_, vl = rot(k_lo, d), rot(b_lo, d), rot(v_lo, d)
        add_terms(acc_lo, q_lo, b_lo, kl, bl_, vl, ok)
        add_terms(acc_hi, q_hi, b_hi, kl, bl_, vl, ok)
        if d == 0:
            add_terms(acc_hi, q_hi, b_hi, k_hi, b_hi, v_hi, None)
        else:
            add_terms(acc_hi, q_hi, b_hi, jnp.where(ok, rot(k_hi, d), kl),
                      jnp.where(ok, rot(b_hi, d), bl_), jnp.where(ok, rot(v_hi, d), vl), None)
    o_intra = [jnp.stack([lo_, hi_], axis=1).reshape(tb, HEAD_DIM)
               for lo_, hi_ in zip(acc_lo, acc_hi)]

    n_sub = tb // HG_SUB
    eb = jnp.exp2(b)
    qe = (q * eb).astype(BF16)
    incs, scales = [], []
    for j in range(n_sub):
        rows = slice(j * HG_SUB, (j + 1) * HG_SUB)
        bl = b[(j + 1) * HG_SUB - 1:(j + 1) * HG_SUB, :]
        ke = (kk[rows, :] * jnp.exp2(bl - b[rows, :])).astype(BF16)
        vb = v[rows, :].astype(BF16)
        incs.append([_dot_tn(vb[:, sl], ke[:, sl]) for sl in heads])
        scales.append(eb[(j + 1) * HG_SUB - 1:(j + 1) * HG_SUB, :])
    st = [state_t[i] for i in range(hb)]
    before = []
    for j in range(n_sub):
        before.append([x.astype(BF16) for x in st])
        st = [x * scales[j][:, sl] + inc for x, sl, inc in zip(st, heads, incs[j])]
    nw = nw_ref[...]
    for i, sl in enumerate(heads):
        outs = [_dot_nt(qe[j * HG_SUB:(j + 1) * HG_SUB, sl], before[j][i]) for j in range(n_sub)]
        state_t[i] = st[i]
        o_ref[:, sl] = _rms_gate(o_intra[i] + jnp.concatenate(outs, axis=0), nw, pz_ref[:, sl])

    @pl.when(t == pl.num_programs(2) - 1)
    def _():
        for i in range(hb):
            s_ref[i] = st[i].T


def _hgrn_prompt(projb, lb_logits, norm_w, batch, seq, n_heads):
    tb = _pick(seq, TIME_BLOCK, HG_SUB)
    nt = seq // tb
    hb = _pick(n_heads, HG_HEADS_PER_STEP, 1)
    ng = n_heads // hb
    d = n_heads * HEAD_DIM
    n_lb = lb_logits.shape[0]
    width = hb * HEAD_DIM

    def pspec(seg):
        return pl.BlockSpec((tb, width), lambda b, h, t: (b * nt + t, seg * ng + h))

    return pl.pallas_call(
        _hgrn_prompt_kernel,
        grid=(batch, ng, nt),
        in_specs=[pspec(1), pspec(2), pspec(3), pspec(4),
                  pl.BlockSpec((n_lb, width), lambda b, h, t: (0, h)),
                  pl.BlockSpec((1, HEAD_DIM), lambda b, h, t: (0, 0))],
        out_specs=[pl.BlockSpec((tb, width), lambda b, h, t: (b * nt + t, h)),
                   pl.BlockSpec((None, hb, HEAD_DIM, HEAD_DIM), lambda b, h, t: (b, h, 0, 0))],
        out_shape=[jax.ShapeDtypeStruct((batch * seq, d), F32),
                   jax.ShapeDtypeStruct((batch, n_heads, HEAD_DIM, HEAD_DIM), F32)],
        scratch_shapes=[pltpu.VMEM((hb, HEAD_DIM, HEAD_DIM), F32)],
        compiler_params=_params(3),
        name="hgrn_prompt",
    )(projb, projb, projb, projb, lb_logits, norm_w)


def _sample_prep_kernel(pq_ref, pk_ref, pv_ref, cq_ref, ck_ref, cv_ref, wq_ref, wk_ref, wv_ref,
                        g_ref, hq_ref, hf_ref, lb_ref,
                        q_ref, k_ref, v_ref, eg_ref, beta_ref, f_ref, qh_ref):
    h = pl.program_id(0)

    def conv(u_ref, c_ref, w_ref):
        w = w_ref[...]
        y = u_ref[...] * w[3:4, :]
        for j in range(3):
            y = y + c_ref[:, j, :] * w[j:j + 1, :]
        return _silu(y)

    q_ref[...] = _l2norm(conv(pq_ref, cq_ref, wq_ref)) * (HEAD_DIM ** -0.5)
    k_ref[...] = _l2norm(conv(pk_ref, ck_ref, wk_ref))
    v_ref[...] = conv(pv_ref, cv_ref, wv_ref)
    gall = g_ref[...]
    shape = q_ref.shape
    eg_ref[...] = jnp.broadcast_to(jnp.exp(_lane_pick(gall, h)), shape)
    beta_ref[...] = jnp.broadcast_to(_lane_pick(gall, h + pl.num_programs(0)), shape)
    lb = _lower_bound(lb_ref[...])
    f_ref[...] = lb + (1.0 - lb) * _sigmoid(hf_ref[...])
    qh_ref[...] = _silu(hq_ref[...]) * (HEAD_DIM ** -0.5)


def _sample_prep(proj, projb, conv_state, conv_w, gates, lb_logits, n_prompt, n_sample, n_heads):
    assert n_prompt % n_sample == 0
    rb = n_prompt // n_sample
    d = n_heads * HEAD_DIM
    n_lb = lb_logits.shape[0]

    def pspec(seg):
        return pl.BlockSpec((n_sample, HEAD_DIM), lambda h: (rb, seg * n_heads + h))

    def cspec(seg):
        return pl.BlockSpec((n_sample, 3, HEAD_DIM), lambda h: (0, 0, seg * n_heads + h))

    def wspec(seg):
        return pl.BlockSpec((4, HEAD_DIM), lambda h: (0, seg * n_heads + h))

    ospec = pl.BlockSpec((n_sample, HEAD_DIM), lambda h: (0, h))
    oshape = jax.ShapeDtypeStruct((n_sample, d), F32)
    return pl.pallas_call(
        _sample_prep_kernel,
        grid=(n_heads,),
        in_specs=[pspec(0), pspec(1), pspec(2), cspec(0), cspec(1), cspec(2),
                  wspec(0), wspec(1), wspec(2),
                  pl.BlockSpec((n_sample, LANES), lambda h: (rb, 0)),
                  pspec(1), pspec(2),
                  pl.BlockSpec((n_lb, HEAD_DIM), lambda h: (0, h))],
        out_specs=[ospec] * 7,
        out_shape=[oshape] * 7,
        compiler_params=_params(1),
        name="sample_prep",
    )(proj, proj, proj, conv_state, conv_state, conv_state, conv_w, conv_w, conv_w,
      gates, projb, projb, lb_logits)


def _gdn_step_kernel(s_ref, qt_ref, kt_ref, v_ref, eg_ref, beta_ref, pz_ref, nw_ref,
                     so_ref, o_ref, obuf):
    i = pl.program_id(0)
    bt = v_ref.shape[0]
    qt = qt_ref[...]
    kt = kt_ref[...]

    def body(bb, carry):
        bg = i * bt + bb
        kcol = _lane_pick(kt, bg)
        qcol = _lane_pick(qt, bg)
        sd = s_ref[bb] * eg_ref[pl.ds(bb, 1), :]
        ks = jnp.sum(sd * kcol, axis=0, keepdims=True)
        u = beta_ref[pl.ds(bb, 1), :] * (v_ref[pl.ds(bb, 1), :] - ks)
        sn = sd + kcol * u
        so_ref[bb] = sn
        obuf[pl.ds(bb, 1), :] = jnp.sum(sn * qcol, axis=0, keepdims=True)
        return carry

    lax.fori_loop(0, bt, body, 0)
    o_ref[...] = _rms_gate(obuf[...], nw_ref[...], pz_ref[...])


def _hgrn_step_kernel(s_ref, qt_ref, ft_ref, pi_ref, pz_ref, nw_ref, so_ref, o_ref, obuf):
    i = pl.program_id(0)
    bt = pi_ref.shape[0]
    qt = qt_ref[...]
    ft = ft_ref[...]

    def body(bb, carry):
        bg = i * bt + bb
        fcol = _lane_pick(ft, bg)
        qcol = _lane_pick(qt, bg)
        sn = fcol * s_ref[bb] + (1.0 - fcol) * pi_ref[pl.ds(bb, 1), :]
        so_ref[bb] = sn
        obuf[pl.ds(bb, 1), :] = jnp.sum(sn * qcol, axis=0, keepdims=True)
        return carry

    lax.fori_loop(0, bt, body, 0)
    o_ref[...] = _rms_gate(obuf[...], nw_ref[...], pz_ref[...])


def _sample_step(kind, state, cols_t, rows, projb, norm_w, n_prompt, n_heads):
    n_sample = state.shape[0]
    bt = _pick(n_sample, 16, 8)
    rb = n_prompt // bt
    sspec = pl.BlockSpec((bt, None, HEAD_DIM, HEAD_DIM), lambda i, h: (i, h, 0, 0))
    tspec = pl.BlockSpec((HEAD_DIM, n_sample), lambda i, h: (h, 0))
    rspec = pl.BlockSpec((bt, HEAD_DIM), lambda i, h: (i, h))

    def pspec(seg):
        return pl.BlockSpec((bt, HEAD_DIM), lambda i, h: (rb + i, seg * n_heads + h))

    nspec = pl.BlockSpec((1, HEAD_DIM), lambda i, h: (0, 0))
    if kind == "gdn":
        kern = _gdn_step_kernel
        in_specs = [sspec, tspec, tspec, rspec, rspec, rspec, pspec(0), nspec]
        args = [state, *cols_t, *rows, projb, norm_w]
    else:
        kern = _hgrn_step_kernel
        in_specs = [sspec, tspec, tspec, pspec(3), pspec(4), nspec]
        args = [state, *cols_t, projb, projb, norm_w]
    return pl.pallas_call(
        kern,
        grid=(n_sample // bt, n_heads),
        in_specs=in_specs,
        out_specs=[sspec, rspec],
        out_shape=[jax.ShapeDtypeStruct(state.shape, F32),
                   jax.ShapeDtypeStruct((n_sample, n_heads * HEAD_DIM), F32)],
        scratch_shapes=[pltpu.VMEM((bt, HEAD_DIM), F32)],
        compiler_params=_params(2),
        name=kind + "_step",
    )(*args)


def _out_proj_kernel(ra_ref, rb_ref, oa_ref, oas_ref, ob_ref, obs_ref, x_ref, w_ref, g_ref, b_ref,
                     h_ref, *, alpha, n_prompt_tiles):
    is_prompt = pl.program_id(0) < n_prompt_tiles
    oa = jnp.where(is_prompt, oa_ref[...], oas_ref[...])
    ob = jnp.where(is_prompt, ob_ref[...], obs_ref[...])
    merged = _sigmoid(ra_ref[...]) * oa + _sigmoid(rb_ref[...]) * ob
    mix = _dot(merged.astype(BF16), w_ref[...])
    h_ref[...] = _layer_norm(alpha * x_ref[...] + mix, g_ref[...], b_ref[...])


def _out_projection(projb, oa, oa_s, ob, ob_s, x, w_out, ln_g, ln_b, alpha):
    n, d = x.shape
    n_prompt = oa.shape[0]
    tm = oa_s.shape[0]
    assert n_prompt % tm == 0 and n == n_prompt + tm
    npt = n_prompt // tm
    row = pl.BlockSpec((tm, d), lambda i: (i, 0))
    prow = pl.BlockSpec((tm, d), lambda i: (jnp.minimum(i, npt - 1), 0))
    srow = pl.BlockSpec((tm, d), lambda i: (0, 0))
    vec = pl.BlockSpec((1, d), lambda i: (0, 0))
    return pl.pallas_call(
        functools.partial(_out_proj_kernel, alpha=alpha, n_prompt_tiles=npt),
        grid=(n // tm,),
        in_specs=[pl.BlockSpec((tm, d), lambda i: (i, 5)),
                  pl.BlockSpec((tm, d), lambda i: (i, 6)),
                  prow, srow, prow, srow, row,
                  pl.BlockSpec((d, d), lambda i: (0, 0)), vec, vec],
        out_specs=row,
        out_shape=jax.ShapeDtypeStruct((n, d), F32),
        compiler_params=_params(1),
        name="out_projection",
    )(projb, projb, oa, oa_s, ob, ob_s, x, w_out, ln_g, ln_b)


def _router_kernel(h_ref, w_ref, b_ref, r_ref, cnt_ref, carry, *, n_experts):
    i = pl.program_id(0)

    @pl.when(i == 0)
    def _():
        carry[...] = jnp.zeros_like(carry)

    logits = _dot3(h_ref[...], w_ref[...]) + b_ref[...]
    tm = logits.shape[0]
    lane = _iota(logits.shape, 1)
    x = jnp.where(lane < n_experts, logits, -jnp.inf)
    vals, idxs = [], []
    for _ in range(TOP_K):
        m = jnp.max(x, axis=1, keepdims=True)
        idx = jnp.min(jnp.where(x == m, lane, LANES), axis=1, keepdims=True)
        vals.append(m)
        idxs.append(idx)
        x = jnp.where(lane == idx, -jnp.inf, x)
    es = [jnp.exp(v - vals[0]) for v in vals]
    den = es[0] + es[1] + es[2] + es[3]
    hot = jnp.zeros(logits.shape, F32)
    for idx in idxs:
        hot = hot + jnp.where(lane == idx, 1.0, 0.0)
    r = _iota((tm, tm), 0)
    c = _iota((tm, tm), 1)
    before = _dot(jnp.where(r > c, 1.0, 0.0).astype(BF16), hot.astype(BF16)) + carry[...]
    out = jnp.zeros(logits.shape, F32)
    for k in range(TOP_K):
        rank = jnp.sum(jnp.where(lane == idxs[k], before, 0.0), axis=1, keepdims=True)
        out = out + jnp.where(lane == k, idxs[k].astype(F32), 0.0)
        out = out + jnp.where(lane == TOP_K + k, rank, 0.0)
        out = out + jnp.where(lane == 2 * TOP_K + k, es[k] / den, 0.0)
    r_ref[...] = out
    carry[...] = carry[...] + jnp.sum(hot, axis=0, keepdims=True)
    cnt_ref[...] = carry[...]


def _router(h, w_r, b_r, n_experts):
    n, d = h.shape
    tm = _pick(n, 208, 16)
    return pl.pallas_call(
        functools.partial(_router_kernel, n_experts=n_experts),
        grid=(n // tm,),
        in_specs=[pl.BlockSpec((tm, d), lambda i: (i, 0)),
                  pl.BlockSpec((d, LANES), lambda i: (0, 0)),
                  pl.BlockSpec((1, LANES), lambda i: (0, 0))],
        out_specs=[pl.BlockSpec((tm, LANES), lambda i: (i, 0)),
                   pl.BlockSpec((1, LANES), lambda i: (0, 0))],
        out_shape=[jax.ShapeDtypeStruct((n, LANES), F32),
                   jax.ShapeDtypeStruct((1, LANES), F32)],
        scratch_shapes=[pltpu.VMEM((1, LANES), F32)],
        compiler_params=_params(1),
        name="router",
    )(h, w_r, b_r)


def _dispatch_kernel(dest_ref, h_ref, xin_ref, xb_ref, sem):
    del xin_ref
    i = pl.program_id(0)
    tm = h_ref.shape[0]

    def row_copy(r, k):
        dst = dest_ref[(i * tm + r) * TOP_K + k]
        return pltpu.make_async_copy(h_ref.at[pl.ds(r, 1), :], xb_ref.at[pl.ds(dst, 1), :], sem)

    def start(r, carry):
        for k in range(TOP_K):
            row_copy(r, k).start()
        return carry

    def wait(r, carry):
        for k in range(TOP_K):
            row_copy(r, k).wait()
        return carry

    lax.fori_loop(0, tm, start, 0)
    lax.fori_loop(0, tm, wait, 0)


def _dispatch(dest_flat, h, xb_init):
    n, d = h.shape
    tm = _pick(n, 128, 8)
    return pl.pallas_call(
        _dispatch_kernel,
        grid_spec=pltpu.PrefetchScalarGridSpec(
            num_scalar_prefetch=1,
            grid=(n // tm,),
            in_specs=[pl.BlockSpec((tm, d), lambda i, dest: (i, 0)),
                      pl.BlockSpec(memory_space=pl.ANY)],
            out_specs=pl.BlockSpec(memory_space=pl.ANY),
            scratch_shapes=[pltpu.SemaphoreType.DMA(())]),
        out_shape=jax.ShapeDtypeStruct(xb_init.shape, F32),
        input_output_aliases={2: 0},
        compiler_params=_params(1),
        name="dispatch",
    )(dest_flat, h, xb_init)


def _deinterleave_matrix(n):
    r = _iota((n, n), 0)
    c = _iota((n, n), 1)
    src = jnp.where(c < n // 2, 2 * c, 2 * (c - n // 2) + 1)
    return jnp.where(r == src, 1.0, 0.0).astype(BF16)


def _moe_kernel(tile_ref, exp_ref, blk0_ref, nblk_ref, zero_ref, x_ref, wgu_ref, bgu_ref, wdn_ref,
                bdn_ref, y_ref, wgu_b, wdn_b):
    del tile_ref, exp_ref
    it = pl.program_id(0)
    j = pl.program_id(1)
    blk0 = blk0_ref[it]
    nblk = nblk_ref[it]
    sub = MOE_COL_TILE // 2

    @pl.when(jnp.logical_and(j == 0, zero_ref[it] == 1))
    def _():
        y_ref[...] = jnp.zeros_like(y_ref)

    @pl.when(nblk > 0)
    def _():
        wgu_b[...] = wgu_ref[...].astype(BF16)
        wdn_b[...] = wdn_ref[...].astype(BF16)
        perm = _deinterleave_matrix(sub)
        bgu = bgu_ref[...]
        bdn = bdn_ref[...]

        def body(bi, carry):
            r0 = pl.multiple_of((blk0 + bi) * MOE_ROW_BLOCK, MOE_ROW_BLOCK)
            xs = x_ref[pl.ds(r0, MOE_ROW_BLOCK), :].astype(BF16)
            hfull = _dot(xs, wgu_b[...]) + bgu
            acts = []
            for s in range(MOE_COL_TILE // sub):
                hs = hfull[:, s * sub:(s + 1) * sub]
                gu = _dot(hs.astype(BF16), perm)
                gate = jnp.minimum(gu[:, 0:sub // 2], SWIGLU_LIMIT)
                up = jnp.clip(gu[:, sub // 2:sub], -SWIGLU_LIMIT, SWIGLU_LIMIT)
                acts.append((up + 1.0) * (gate * _sigmoid(gate * SWIGLU_ALPHA)))
            act = jnp.concatenate(acts, axis=1).astype(BF16)
            contrib = _dot(act, wdn_b[...])
            prev = jnp.where(j == 0, jnp.broadcast_to(bdn, contrib.shape),
                             y_ref[pl.ds(r0, MOE_ROW_BLOCK), :])
            y_ref[pl.ds(r0, MOE_ROW_BLOCK), :] = prev + contrib
            return carry

        lax.fori_loop(0, nblk, body, 0)


def _moe_gemm(item_tile, item_exp, item_blk0, item_nblk, item_zero, xb, w_gu, b_gu, w_dn, b_dn):
    p_rows = xb.shape[0]
    n_exp, d, two_de = w_gu.shape
    de = two_de // 2
    n_items = item_tile.shape[0]
    nj = two_de // MOE_COL_TILE
    dn_rows = MOE_COL_TILE // 2

    def live_j(it, j, nblk):
        return jnp.where(nblk[it] > 0, j, nj - 1)

    return pl.pallas_call(
        _moe_kernel,
        grid_spec=pltpu.PrefetchScalarGridSpec(
            num_scalar_prefetch=5,
            grid=(n_items, nj),
            in_specs=[
                pl.BlockSpec((MOE_ROW_TILE, d), lambda it, j, tl, ex, b0, nb, zf: (tl[it], 0)),
                pl.BlockSpec((None, d, MOE_COL_TILE),
                             lambda it, j, tl, ex, b0, nb, zf: (ex[it], 0, live_j(it, j, nb))),
                pl.BlockSpec((None, 1, MOE_COL_TILE),
                             lambda it, j, tl, ex, b0, nb, zf: (ex[it], 0, live_j(it, j, nb))),
                pl.BlockSpec((None, dn_rows, d),
                             lambda it, j, tl, ex, b0, nb, zf: (ex[it], live_j(it, j, nb), 0)),
                pl.BlockSpec((None, 1, d), lambda it, j, tl, ex, b0, nb, zf: (ex[it], 0, 0)),
            ],
            out_specs=pl.BlockSpec((MOE_ROW_TILE, d), lambda it, j, tl, ex, b0, nb, zf: (tl[it], 0)),
            scratch_shapes=[pltpu.VMEM((d, MOE_COL_TILE), BF16),
                            pltpu.VMEM((dn_rows, d), BF16)]),
        out_shape=jax.ShapeDtypeStruct((p_rows, d), F32),
        compiler_params=_params(2),
        name="moe_gemm",
    )(item_tile, item_exp, item_blk0, item_nblk, item_zero, xb, w_gu, b_gu.reshape(n_exp, 1, two_de),
      w_dn, b_dn.reshape(n_exp, 1, d))


def _combine_kernel(dest_ref, h_ref, r_ref, g_ref, b_ref, yb_ref, o_ref, rows, sem, *, alpha):
    i = pl.program_id(0)
    tm, d = h_ref.shape

    def row_copy(r, k):
        src = dest_ref[(i * tm + r) * TOP_K + k]
        return pltpu.make_async_copy(yb_ref.at[pl.ds(src, 1), :], rows.at[k, pl.ds(r, 1), :], sem)

    def start(r, carry):
        for k in range(TOP_K):
            row_copy(r, k).start()
        return carry

    def wait(r, carry):
        for k in range(TOP_K):
            row_copy(r, k).wait()
        return carry

    lax.fori_loop(0, tm, start, 0)
    lax.fori_loop(0, tm, wait, 0)
    rr = r_ref[...]
    ffn = jnp.zeros((tm, d), F32)
    for k in range(TOP_K):
        ffn = ffn + _lane_pick(rr, 2 * TOP_K + k) * rows[k]
    o_ref[...] = _layer_norm(alpha * h_ref[...] + ffn, g_ref[...], b_ref[...])


def _combine(dest_flat, h, route, ln_g, ln_b, yb, alpha):
    n, d = h.shape
    tm = _pick(n, 128, 8)
    return pl.pallas_call(
        functools.partial(_combine_kernel, alpha=alpha),
        grid_spec=pltpu.PrefetchScalarGridSpec(
            num_scalar_prefetch=1,
            grid=(n // tm,),
            in_specs=[pl.BlockSpec((tm, d), lambda i, dest: (i, 0)),
                      pl.BlockSpec((tm, LANES), lambda i, dest: (i, 0)),
                      pl.BlockSpec((1, d), lambda i, dest: (0, 0)),
                      pl.BlockSpec((1, d), lambda i, dest: (0, 0)),
                      pl.BlockSpec(memory_space=pl.ANY)],
            out_specs=pl.BlockSpec((tm, d), lambda i, dest: (i, 0)),
            scratch_shapes=[pltpu.VMEM((TOP_K, tm, d), F32),
                            pltpu.SemaphoreType.DMA(())]),
        out_shape=jax.ShapeDtypeStruct((n, d), F32),
        compiler_params=_params(1),
        name="combine",
    )(dest_flat, h, route, ln_g, ln_b, yb)


def _routing_tables(route, counts_row, n_experts):
    n = route.shape[0]
    e_idx = route[:, 0:TOP_K].astype(I32)
    rank = route[:, TOP_K:2 * TOP_K].astype(I32)
    counts = counts_row[0, :n_experts].astype(I32)
    pcounts = (counts + MOE_ROW_BLOCK - 1) // MOE_ROW_BLOCK * MOE_ROW_BLOCK
    pends = jnp.cumsum(pcounts)
    pstarts = pends - pcounts
    dest = (pstarts[e_idx] + rank).reshape(-1)

    p_rows = -(-(n * TOP_K + n_experts * MOE_ROW_BLOCK) // MOE_ROW_TILE) * MOE_ROW_TILE
    n_blocks = p_rows // MOE_ROW_BLOCK
    bpt = MOE_ROW_TILE // MOE_ROW_BLOCK
    max_items = p_rows // MOE_ROW_TILE + n_experts
    blk = jnp.arange(n_blocks, dtype=I32)
    valid = blk * MOE_ROW_BLOCK < pends[-1]
    blk_e = jnp.minimum(jnp.sum((pends[None, :] <= (blk * MOE_ROW_BLOCK)[:, None]).astype(I32), axis=1),
                        n_experts - 1)
    first = valid & ((blk % bpt == 0) | (blk_e != jnp.roll(blk_e, 1)))
    item_of_blk = jnp.cumsum(first.astype(I32)) - 1
    n_items = jnp.sum(first.astype(I32))
    it = jnp.arange(max_items, dtype=I32)
    mine = item_of_blk[None, :] == it[:, None]
    item_first = jnp.sum(jnp.where(mine & first[None, :], blk[None, :], 0), axis=1)
    item_nblk = jnp.sum((mine & valid[None, :]).astype(I32), axis=1)
    live = it < n_items
    item_first = item_first[jnp.minimum(it, n_items - 1)]
    item_nblk = jnp.where(live, item_nblk, 0)
    item_blk0 = item_first % bpt
    n_tiles = p_rows // MOE_ROW_TILE
    last_tile = item_first[max_items - 1] // bpt
    idle_tile = last_tile + 1 + (it - n_items)
    item_tile = jnp.where(live, item_first // bpt, jnp.minimum(idle_tile, n_tiles - 1))
    item_zero = jnp.where(live, (item_blk0 == 0) & (item_nblk > 0), idle_tile < n_tiles)
    return (dest, p_rows, item_tile, blk_e[item_first], item_blk0, item_nblk,
            item_zero.astype(I32))


def kernel(x_prompt, x_sample, state_gdn_conv, state_gdn_s, state_hgrn_s, w_in, gdn_conv_w,
           gdn_a_log, gdn_dt_bias, gdn_norm_w, hg_lb_logits, hg_norm_w, w_out, ln1_g, ln1_b,
           w_router, b_router, w_gate_up, b_gate_up, w_down, b_down, ln2_g, ln2_b):
    depth = w_in.shape[0]
    assert depth == 1
    batch, seq, d = x_prompt.shape
    n_sample = x_sample.shape[0]
    assert x_sample.shape[1] == 1
    n_heads = d // HEAD_DIM
    n_prompt = batch * seq
    n_rows = n_prompt + n_sample
    n_experts = w_router.shape[-1]
    alpha = (2.0 * depth) ** 0.25
    qkv = 3 * d
    ab0 = qkv
    ab1 = qkv + 2 * n_heads

    x_all = jnp.concatenate([x_prompt.reshape(n_prompt, d), x_sample.reshape(n_sample, d)],
                            axis=0).astype(F32)
    w = w_in[0]
    w_rest = w[:, ab1:].astype(BF16)
    w_ab = jnp.pad(w[:, ab0:ab1].astype(F32), ((0, 0), (0, LANES - 2 * n_heads)))
    prm = jnp.zeros((8, LANES), F32)
    prm = prm.at[0, :n_heads].set(gdn_a_log[0].astype(F32))
    prm = prm.at[1, :n_heads].set(gdn_dt_bias[0].astype(F32))
    conv_w = gdn_conv_w[0].astype(F32)
    gdn_nw = gdn_norm_w[0].astype(F32).reshape(1, HEAD_DIM)
    hg_nw = hg_norm_w[0].astype(F32).reshape(1, HEAD_DIM)
    lb_logits = hg_lb_logits.astype(F32)

    proj = _in_projection(x_all, w, qkv, 512)
    projb = _in_projection(x_all, w_rest, w_rest.shape[1], 1024)
    gates, gates_t = _gates(x_all, w_ab, prm, n_prompt, n_heads)
    oa, gdn_s_prompt = _gdn_prompt(proj, projb, gates, gates_t, conv_w, gdn_nw, batch, seq, n_heads)
    ob, hg_s_prompt = _hgrn_prompt(projb, lb_logits, hg_nw, batch, seq, n_heads)

    conv_state = state_gdn_conv[0].astype(F32)
    sq, sk, sv, seg, sbeta, sf, sqh = _sample_prep(proj, projb, conv_state, conv_w, gates,
                                                   lb_logits,
                                                   n_prompt, n_sample, n_heads)
    gdn_s_sample, oa_s = _sample_step("gdn", state_gdn_s[0].astype(F32), (sq.T, sk.T),
                                      (sv, seg, sbeta), projb, gdn_nw, n_prompt, n_heads)
    hg_s_sample, ob_s = _sample_step("hgrn", state_hgrn_s[0].astype(F32), (sqh.T, sf.T), (),
                                     projb, hg_nw, n_prompt, n_heads)

    h = _out_projection(projb, oa, oa_s, ob, ob_s, x_all, w_out[0].astype(BF16),
                        ln1_g[0].astype(F32).reshape(1, d), ln1_b[0].astype(F32).reshape(1, d),
                        alpha)

    w_r = jnp.pad(w_router[0].astype(F32), ((0, 0), (0, LANES - n_experts)))
    b_r = jnp.pad(b_router[0].astype(F32), (0, LANES - n_experts)).reshape(1, LANES)
    route, counts_row = _router(h, w_r, b_r, n_experts)
    dest, p_rows, item_tile, item_exp, item_blk0, item_nblk, item_zero = _routing_tables(
        route, counts_row, n_experts)

    xb = _dispatch(dest, h, jnp.zeros((p_rows, d), F32))
    yb = _moe_gemm(item_tile, item_exp, item_blk0, item_nblk, item_zero, xb, w_gate_up[0],
                   b_gate_up[0], w_down[0], b_down[0])
    y = _combine(dest, h, route, ln2_g[0].astype(F32).reshape(1, d),
                 ln2_b[0].astype(F32).reshape(1, d), yb, alpha)

    y_prompt = y[:n_prompt].reshape(batch, seq, d).astype(x_prompt.dtype)
    y_sample = y[n_prompt:].reshape(n_sample, 1, d).astype(x_sample.dtype)
    new_conv_p = jnp.stack([proj[(b + 1) * seq - 3:(b + 1) * seq] for b in range(batch)])[None]
    new_conv_s = jnp.concatenate([conv_state[:, 1:, :], proj[n_prompt:, None, :]], axis=1)[None]
    sdt = state_gdn_s.dtype
    return (y_prompt, y_sample,
            new_conv_p.astype(state_gdn_conv.dtype), gdn_s_prompt[None].astype(sdt),
            hg_s_prompt[None].astype(state_hgrn_s.dtype),
            new_conv_s.astype(state_gdn_conv.dtype), gdn_s_sample[None].astype(sdt),
            hg_s_sample[None].astype(state_hgrn_s.dtype))
```

```python
import functools

import jax
import jax.numpy as jnp
from jax import lax
from jax.experimental import pallas as pl
from jax.experimental.pallas import tpu as pltpu

F32 = jnp.float32
BF16 = jnp.bfloat16
I32 = jnp.int32
U32 = jnp.uint32

HEAD_DIM = 128
LANES = 128
GDN_CHUNK = 64
HG_SUB = 16
TIME_BLOCK = 256
GDN_HEADS_PER_STEP = 4
HG_HEADS_PER_STEP = 2
STEP_UNROLL = 4
TOP_K = 4
MOE_ROW_BLOCK = 256
MOE_ROW_TILE = 1024
MOE_COL_TILE = 512
SWIGLU_LIMIT = 7.0
SWIGLU_ALPHA = 1.702
LN_EPS = 1e-5
RMS_EPS = 1e-6
L2_EPS = 1e-6
VMEM_LIMIT = 56 * 1024 * 1024


def _params(n_axes, vmem=VMEM_LIMIT):
    return pltpu.CompilerParams(dimension_semantics=("arbitrary",) * n_axes,
                                vmem_limit_bytes=vmem)


def _pick(n, target, mult):
    best = None
    for d in range(mult, min(n, target) + 1, mult):
        if n % d == 0:
            best = d
    assert best is not None, (n, target, mult)
    return best


def _dot(a, b):
    return jnp.dot(a, b, preferred_element_type=F32)


def _dot_nt(a, b):
    return lax.dot_general(a, b, (((1,), (1,)), ((), ())), preferred_element_type=F32)


def _dot_tn(a, b):
    return lax.dot_general(a, b, (((0,), (0,)), ((), ())), preferred_element_type=F32)


def _hi_lo(x):
    hi = x.astype(BF16)
    lo = (x - hi.astype(F32)).astype(BF16)
    return hi, lo


def _dot3(a, b):
    ah, al = _hi_lo(a)
    bh, bl = _hi_lo(b)
    return _dot(ah, bh) + (_dot(ah, bl) + _dot(al, bh))


def _dot_exact_lhs(m_bf16, x):
    p1 = x.astype(BF16)
    r1 = x - p1.astype(F32)
    p2 = r1.astype(BF16)
    p3 = (r1 - p2.astype(F32)).astype(BF16)
    return _dot(m_bf16, p1) + (_dot(m_bf16, p2) + _dot(m_bf16, p3))


def _sigmoid(x):
    return 1.0 / (1.0 + jnp.exp(-x))


def _silu(x):
    return x * _sigmoid(x)


def _softplus(x):
    return jnp.maximum(x, 0.0) + jnp.log(1.0 + jnp.exp(-jnp.abs(x)))


def _iota(shape, dim):
    return lax.broadcasted_iota(I32, shape, dim)


def _layer_norm(y, g, b):
    mu = jnp.mean(y, axis=-1, keepdims=True)
    yc = y - mu
    var = jnp.mean(yc * yc, axis=-1, keepdims=True)
    return yc * lax.rsqrt(var + LN_EPS) * g + b


def _rms_gate(o, w, z):
    return o * lax.rsqrt(jnp.mean(o * o, axis=-1, keepdims=True) + RMS_EPS) * w * _silu(z)


def _l2norm(x):
    return x * lax.rsqrt(jnp.sum(x * x, axis=-1, keepdims=True) + L2_EPS)


def _lane_pick(x, idx):
    lane = _iota(x.shape, 1)
    return jnp.sum(jnp.where(lane == idx, x, 0.0), axis=1, keepdims=True)


def _mm_nt_kernel(x_ref, wt_ref, o_ref):
    o_ref[...] = _dot_nt(x_ref[...].astype(BF16), wt_ref[...].astype(BF16))


def _in_projection(x, wt, row0, no, tn_target):
    n, d = x.shape
    tm = _pick(n, 1040, 16)
    tn = _pick(no, tn_target, LANES)
    if row0 % tn == 0:
        wspec = pl.BlockSpec((tn, d), lambda i, j: (row0 // tn + j, 0))
    else:
        assert row0 % 8 == 0
        wspec = pl.BlockSpec((pl.Element(tn), pl.Element(d)),
                             lambda i, j: (pl.multiple_of(row0 + j * tn, 8), 0))
    return pl.pallas_call(
        _mm_nt_kernel,
        grid=(n // tm, no // tn),
        in_specs=[pl.BlockSpec((tm, d), lambda i, j: (i, 0)), wspec],
        out_specs=pl.BlockSpec((tm, tn), lambda i, j: (i, j)),
        out_shape=jax.ShapeDtypeStruct((n, no), F32),
        compiler_params=_params(2),
        name="in_projection",
    )(x, wt)


def _gates_kernel(x_ref, w_ref, prm_ref, g_ref, gt_ref, *, n_prompt_tiles, n_heads):
    i = pl.program_id(0)
    xh, xl = _hi_lo(x_ref[...])
    wh, wl = _hi_lo(w_ref[...])
    ab = _dot_nt(xh, wh) + (_dot_nt(xh, wl) + _dot_nt(xl, wh))
    prm = prm_ref[...]
    g = -jnp.exp(prm[0:1]) * _softplus(ab + prm[1:2])
    beta = _sigmoid(ab)
    tm = ab.shape[0]
    r = _iota((tm, tm), 0)
    c = _iota((tm, tm), 1)
    shift = jnp.where(i < n_prompt_tiles, GDN_CHUNK.bit_length() - 1, 0)
    tri = jnp.logical_and((r >> shift) == (c >> shift), r >= c)
    gc = _dot_exact_lhs(jnp.where(tri, 1.0, 0.0).astype(BF16), g)
    lane = _iota(ab.shape, 1)
    out = jnp.where(lane < n_heads, gc, beta)
    g_ref[...] = out
    gt_ref[...] = out.T


def _gates(x, w_ab, prm, n_prompt, n_heads):
    n, d = x.shape
    tm = LANES
    assert n % tm == 0 and n_prompt % tm == 0 and tm % GDN_CHUNK == 0
    kern = functools.partial(_gates_kernel, n_prompt_tiles=n_prompt // tm, n_heads=n_heads)
    return pl.pallas_call(
        kern,
        grid=(n // tm,),
        in_specs=[pl.BlockSpec((tm, d), lambda i: (i, 0)),
                  pl.BlockSpec((LANES, d), lambda i: (0, 0)),
                  pl.BlockSpec((8, LANES), lambda i: (0, 0))],
        out_specs=[pl.BlockSpec((tm, LANES), lambda i: (i, 0)),
                   pl.BlockSpec((LANES, tm), lambda i: (0, i))],
        out_shape=[jax.ShapeDtypeStruct((n, LANES), F32),
                   jax.ShapeDtypeStruct((LANES, n), F32)],
        compiler_params=_params(1),
        name="gates",
    )(x, w_ab, prm)


def _split_dot3(ah, al, bh, bl):
    return _dot(ah, bh) + (_dot(ah, bl) + _dot(al, bh))


def _unit_lower_inverses(mats):
    c = mats[0].shape[0]
    r = _iota((c, c), 0)
    col = _iota((c, c), 1)
    eye = jnp.where(r == col, 1.0, 0.0)
    pair = jnp.logical_and((r >> 1) == (col >> 1), r > col)
    invs = [eye - jnp.where(pair, a, 0.0) for a in mats]
    level = 2
    while (1 << level) <= c:
        half = level - 1
        mask = jnp.logical_and(
            (r >> level) == (col >> level),
            jnp.logical_and(((r >> half) & 1) == 1, ((col >> half) & 1) == 0))
        lows = [_hi_lo(jnp.where(mask, a, 0.0)) for a in mats]
        inv_s = [_hi_lo(inv) for inv in invs]
        xs = [_split_dot3(lh, ll, ih, il) for (lh, ll), (ih, il) in zip(lows, inv_s)]
        x_s = [_hi_lo(x) for x in xs]
        invs = [inv - _split_dot3(ih, il, xh, xl)
                for inv, (ih, il), (xh, xl) in zip(invs, inv_s, x_s)]
        level += 1
    return invs


def _gdn_prompt_kernel(pq_ref, pk_ref, pv_ref, pz_ref, g_ref, gt_ref, wq_ref, wk_ref, wv_ref,
                       nw_ref, o_ref, s_ref, state, cbuf, qs, ks, vs):
    hg = pl.program_id(1)
    t = pl.program_id(2)
    tb = pq_ref.shape[0]
    hb = state.shape[0]
    n_heads = pl.num_programs(1) * hb
    cl = GDN_CHUNK

    @pl.when(t == 0)
    def _():
        state[...] = jnp.zeros_like(state)
        cbuf[:, 0:8, :] = jnp.zeros((3, 8, hb * HEAD_DIM), F32)

    def conv(idx, u_ref, w_ref):
        u = u_ref[...]
        cbuf[idx, 8:8 + tb, :] = u
        w = w_ref[...]
        y = cbuf[idx, 5:5 + tb, :] * w[0:1, :]
        for j in range(1, 4):
            y = y + cbuf[idx, 5 + j:5 + j + tb, :] * w[j:j + 1, :]
        cbuf[idx, 0:8, :] = u[tb - 8:tb, :]
        return _silu(y)

    qc = conv(0, pq_ref, wq_ref)
    kc = conv(1, pk_ref, wk_ref)
    vs[...] = conv(2, pv_ref, wv_ref)
    for i in range(hb):
        sl = slice(i * HEAD_DIM, (i + 1) * HEAD_DIM)
        qs[:, sl] = _l2norm(qc[:, sl]) * (HEAD_DIM ** -0.5)
        ks[:, sl] = _l2norm(kc[:, sl])

    gall = g_ref[...]
    gt = gt_ref[...]
    gc_cols = [_lane_pick(gall, hg * hb + i) for i in range(hb)]
    beta_cols = [_lane_pick(gall, n_heads + hg * hb + i) for i in range(hb)]

    r = _iota((cl, cl), 0)
    col = _iota((cl, cl), 1)
    incl = r >= col
    strict = r > col

    pairs = [(c, i) for c in range(tb // cl) for i in range(hb)]
    pre = []
    for c, i in pairs:
        rows = slice(c * cl, (c + 1) * cl)
        sl = slice(i * HEAD_DIM, (i + 1) * HEAD_DIM)
        q = qs[rows, sl]
        k = ks[rows, sl]
        v = vs[rows, sl]
        gcc = gc_cols[i][rows, :]
        gcr = gt[i:i + 1, rows]
        bc = beta_cols[i][rows, :]
        decay = jnp.where(incl, jnp.exp(jnp.where(incl, gcc - gcr, 0.0)), 0.0)
        kb = k.astype(BF16)
        a = jnp.where(strict, bc * decay * _dot_nt(kb, kb), 0.0)
        aqk = (_dot_nt(q.astype(BF16), kb) * decay).astype(BF16)
        egc = jnp.exp(gcc)
        g_last = gcc[cl - 1:cl, :]
        pre.append(dict(a=a, aqk=aqk, rhs=_hi_lo(jnp.concatenate([bc * egc * k, bc * v], axis=1)),
                        qd=(q * egc).astype(BF16), ke=(k * jnp.exp(g_last - gcc)).astype(BF16),
                        g_end=jnp.exp(g_last)))
    invs = _unit_lower_inverses([p["a"] for p in pre])
    for p, inv in zip(pre, invs):
        ih, il = _hi_lo(inv)
        wu = _split_dot3(ih, il, *p["rhs"])
        p["w"] = wu[:, :HEAD_DIM].astype(BF16)
        p["u0"] = wu[:, HEAD_DIM:]

    s = [state[i] for i in range(hb)]
    outs = [[] for _ in range(hb)]
    for c in range(tb // cl):
        ps = [pre[c * hb + i] for i in range(hb)]
        sb = [x.astype(BF16) for x in s]
        ws = [_dot(p["w"], b) for p, b in zip(ps, sb)]
        qsd = [_dot(p["qd"], b) for p, b in zip(ps, sb)]
        ub = [(p["u0"] - x).astype(BF16) for p, x in zip(ps, ws)]
        s = [p["g_end"] * x + _dot_tn(p["ke"], u) for p, x, u in zip(ps, s, ub)]
        for i in range(hb):
            outs[i].append(qsd[i] + _dot(ps[i]["aqk"], ub[i]))
    nw = nw_ref[...]
    for i in range(hb):
        sl = slice(i * HEAD_DIM, (i + 1) * HEAD_DIM)
        state[i] = s[i]
        o_ref[:, sl] = _rms_gate(jnp.concatenate(outs[i], axis=0), nw, pz_ref[:, sl])

    @pl.when(t == pl.num_programs(2) - 1)
    def _():
        for i in range(hb):
            s_ref[i] = s[i]


def _gdn_prompt(proj, projb, gates, gates_t, conv_w, norm_w, batch, seq, n_heads):
    tb = _pick(seq, TIME_BLOCK, GDN_CHUNK)
    nt = seq // tb
    hb = _pick(n_heads, GDN_HEADS_PER_STEP, 1)
    ng = n_heads // hb
    d = n_heads * HEAD_DIM
    n_rows = gates_t.shape[1]
    gates_t3 = gates_t[:n_heads].reshape(ng, hb, n_rows)

    def rows(b, h, t):
        return b * nt + t

    def pspec(seg):
        return pl.BlockSpec((tb, hb * HEAD_DIM), lambda b, h, t: (rows(b, h, t), seg * ng + h))

    def wspec(seg):
        return pl.BlockSpec((4, hb * HEAD_DIM), lambda b, h, t: (0, seg * ng + h))

    return pl.pallas_call(
        _gdn_prompt_kernel,
        grid=(batch, ng, nt),
        in_specs=[pspec(0), pspec(1), pspec(2), pspec(0),
                  pl.BlockSpec((tb, LANES), lambda b, h, t: (rows(b, h, t), 0)),
                  pl.BlockSpec((None, hb, tb), lambda b, h, t: (h, 0, rows(b, h, t))),
                  wspec(0), wspec(1), wspec(2),
                  pl.BlockSpec((1, HEAD_DIM), lambda b, h, t: (0, 0))],
        out_specs=[pl.BlockSpec((tb, hb * HEAD_DIM), lambda b, h, t: (rows(b, h, t), h)),
                   pl.BlockSpec((None, hb, HEAD_DIM, HEAD_DIM), lambda b, h, t: (b, h, 0, 0))],
        out_shape=[jax.ShapeDtypeStruct((batch * seq, d), F32),
                   jax.ShapeDtypeStruct((batch, n_heads, HEAD_DIM, HEAD_DIM), F32)],
        scratch_shapes=[pltpu.VMEM((hb, HEAD_DIM, HEAD_DIM), F32),
                        pltpu.VMEM((3, tb + 8, hb * HEAD_DIM), F32),
                        pltpu.VMEM((tb, hb * HEAD_DIM), F32),
                        pltpu.VMEM((tb, hb * HEAD_DIM), F32),
                        pltpu.VMEM((tb, hb * HEAD_DIM), F32)],
        compiler_params=_params(3),
        name="gdn_prompt",
    )(proj, proj, proj, projb, gates, gates_t3, conv_w, conv_w, conv_w, norm_w)


def _lower_bound(logits):
    m = jnp.max(logits, axis=0, keepdims=True)
    e = jnp.exp(logits - m)
    return e[0:1, :] / jnp.sum(e, axis=0, keepdims=True)


def _hgrn_prompt_kernel(pq_ref, pf_ref, pi_ref, pz_ref, lb_ref, nw_ref, o_ref, s_ref,
                        state_t):
    t = pl.program_id(2)
    tb = pq_ref.shape[0]
    hb = state_t.shape[0]

    @pl.when(t == 0)
    def _():
        state_t[...] = jnp.zeros_like(state_t)

    lb = _lower_bound(lb_ref[...])
    f = lb + (1.0 - lb) * _sigmoid(pf_ref[...])
    kk = 1.0 - f
    lf = jnp.log2(f)
    q = _silu(pq_ref[...]) * (HEAD_DIM ** -0.5)
    v = pi_ref[...]

    r = _iota((tb, tb), 0)
    c = _iota((tb, tb), 1)
    sub_shift = HG_SUB.bit_length() - 1
    tri = jnp.logical_and((r >> sub_shift) == (c >> sub_shift), r >= c)
    b = _dot_exact_lhs(jnp.where(tri, 1.0, 0.0).astype(BF16), lf)

    width = hb * HEAD_DIM
    heads = [slice(i * HEAD_DIM, (i + 1) * HEAD_DIM) for i in range(hb)]

    half = HG_SUB // 2
    n_sub = tb // HG_SUB

    def halves(x):
        x4 = x.reshape(n_sub, 2, half, width)
        return x4[:, 0], x4[:, 1]

    q_lo, q_hi = halves(q)
    b_lo, b_hi = halves(b)
    k_lo, k_hi = halves(kk)
    v_lo, v_hi = halves(v)
    row = _iota((1, half, 1), 1)

    def rot(x, d):
        return x if d == 0 else pltpu.roll(x, d, 1)

    def add_terms(acc, qx, bx, kp, bp, vp, ok):
        prod = qx * kp * jnp.exp2(bx - bp)
        for i, sl in enumerate(heads):
            w = jnp.sum(prod[:, :, sl], axis=2, keepdims=True)
            if ok is not None:
                w = jnp.where(ok, w, 0.0)
            acc[i] = acc[i] + w * vp[:, :, sl]

    acc_lo = [jnp.zeros((n_sub, half, HEAD_DIM), F32) for _ in range(hb)]
    acc_hi = [jnp.zeros((n_sub, half, HEAD_DIM), F32) for _ in range(hb)]
    for d in range(half):
        ok = None if d == 0 else row >= d
        kl, bl_, vl = rot(k_lo, d), rot(b_lo, d), rot(v_lo, d)
        add_terms(acc_lo, q_lo, b_lo, kl, bl_, vl, ok)
        add_terms(acc_hi, q_hi, b_hi, kl, bl_, vl, ok)
        if d == 0:
            add_terms(acc_hi, q_hi, b_hi, k_hi, b_hi, v_hi, None)
        else:
            add_terms(acc_hi, q_hi, b_hi, jnp.where(ok, rot(k_hi, d), kl),
                      jnp.where(ok, rot(b_hi, d), bl_), jnp.where(ok, rot(v_hi, d), vl), None)
    o_intra = [jnp.stack([lo_, hi_], axis=1).reshape(tb, HEAD_DIM)
               for lo_, hi_ in zip(acc_lo, acc_hi)]

    n_sub = tb // HG_SUB
    eb = jnp.exp2(b)
    qe = (q * eb).astype(BF16)
    incs, scales = [], []
    for j in range(n_sub):
        rows = slice(j * HG_SUB, (j + 1) * HG_SUB)
        bl = b[(j + 1) * HG_SUB - 1:(j + 1) * HG_SUB, :]
        ke = (kk[rows, :] * jnp.exp2(bl - b[rows, :])).astype(BF16)
        vb = v[rows, :].astype(BF16)
        incs.append([_dot_tn(vb[:, sl], ke[:, sl]) for sl in heads])
        scales.append(eb[(j + 1) * HG_SUB - 1:(j + 1) * HG_SUB, :])
    st = [state_t[i] for i in range(hb)]
    before = []
    for j in range(n_sub):
        before.append([x.astype(BF16) for x in st])
        st = [x * scales[j][:, sl] + inc for x, sl, inc in zip(st, heads, incs[j])]
    nw = nw_ref[...]
    for i, sl in enumerate(heads):
        outs = [_dot_nt(qe[j * HG_SUB:(j + 1) * HG_SUB, sl], before[j][i]) for j in range(n_sub)]
        state_t[i] = st[i]
        o_ref[:, sl] = _rms_gate(o_intra[i] + jnp.concatenate(outs, axis=0), nw, pz_ref[:, sl])

    @pl.when(t == pl.num_programs(2) - 1)
    def _():
        for i in range(hb):
            s_ref[i] = st[i].T


def _hgrn_prompt(projb, lb_logits, norm_w, batch, seq, n_heads):
    tb = _pick(seq, TIME_BLOCK, HG_SUB)
    nt = seq // tb
    hb = _pick(n_heads, HG_HEADS_PER_STEP, 1)
    ng = n_heads // hb
    d = n_heads * HEAD_DIM
    n_lb = lb_logits.shape[0]
    width = hb * HEAD_DIM

    def pspec(seg):
        return pl.BlockSpec((tb, width), lambda b, h, t: (b * nt + t, seg * ng + h))

    return pl.pallas_call(
        _hgrn_prompt_kernel,
        grid=(batch, ng, nt),
        in_specs=[pspec(1), pspec(2), pspec(3), pspec(4),
                  pl.BlockSpec((n_lb, width), lambda b, h, t: (0, h)),
                  pl.BlockSpec((1, HEAD_DIM), lambda b, h, t: (0, 0))],
        out_specs=[pl.BlockSpec((tb, width), lambda b, h, t: (b * nt + t, h)),
                   pl.BlockSpec((None, hb, HEAD_DIM, HEAD_DIM), lambda b, h, t: (b, h, 0, 0))],
        out_shape=[jax.ShapeDtypeStruct((batch * seq, d), F32),
                   jax.ShapeDtypeStruct((batch, n_heads, HEAD_DIM, HEAD_DIM), F32)],
        scratch_shapes=[pltpu.VMEM((hb, HEAD_DIM, HEAD_DIM), F32)],
        compiler_params=_params(3),
        name="hgrn_prompt",
    )(projb, projb, projb, projb, lb_logits, norm_w)


def _sample_prep_kernel(pq_ref, pk_ref, pv_ref, cq_ref, ck_ref, cv_ref, wq_ref, wk_ref, wv_ref,
                        g_ref, hq_ref, hf_ref, lb_ref,
                        q_ref, k_ref, v_ref, eg_ref, beta_ref, f_ref, qh_ref):
    h = pl.program_id(0)

    def conv(u_ref, c_ref, w_ref):
        w = w_ref[...]
        y = u_ref[...] * w[3:4, :]
        for j in range(3):
            y = y + c_ref[:, j, :] * w[j:j + 1, :]
        return _silu(y)

    q_ref[...] = _l2norm(conv(pq_ref, cq_ref, wq_ref)) * (HEAD_DIM ** -0.5)
    k_ref[...] = _l2norm(conv(pk_ref, ck_ref, wk_ref))
    v_ref[...] = conv(pv_ref, cv_ref, wv_ref)
    gall = g_ref[...]
    shape = q_ref.shape
    eg_ref[...] = jnp.broadcast_to(jnp.exp(_lane_pick(gall, h)), shape)
    beta_ref[...] = jnp.broadcast_to(_lane_pick(gall, h + pl.num_programs(0)), shape)
    lb = _lower_bound(lb_ref[...])
    f_ref[...] = lb + (1.0 - lb) * _sigmoid(hf_ref[...])
    qh_ref[...] = _silu(hq_ref[...]) * (HEAD_DIM ** -0.5)


def _sample_prep(proj, projb, conv_state, conv_w, gates, lb_logits, n_prompt, n_sample, n_heads):
    assert n_prompt % n_sample == 0
    rb = n_prompt // n_sample
    d = n_heads * HEAD_DIM
    n_lb = lb_logits.shape[0]

    def pspec(seg):
        return pl.BlockSpec((n_sample, HEAD_DIM), lambda h: (rb, seg * n_heads + h))

    def cspec(seg):
        return pl.BlockSpec((n_sample, 3, HEAD_DIM), lambda h: (0, 0, seg * n_heads + h))

    def wspec(seg):
        return pl.BlockSpec((4, HEAD_DIM), lambda h: (0, seg * n_heads + h))

    ospec = pl.BlockSpec((n_sample, HEAD_DIM), lambda h: (0, h))
    oshape = jax.ShapeDtypeStruct((n_sample, d), F32)
    return pl.pallas_call(
        _sample_prep_kernel,
        grid=(n_heads,),
        in_specs=[pspec(0), pspec(1), pspec(2), cspec(0), cspec(1), cspec(2),
                  wspec(0), wspec(1), wspec(2),
                  pl.BlockSpec((n_sample, LANES), lambda h: (rb, 0)),
                  pspec(1), pspec(2),
                  pl.BlockSpec((n_lb, HEAD_DIM), lambda h: (0, h))],
        out_specs=[ospec] * 7,
        out_shape=[oshape] * 7,
        compiler_params=_params(1),
        name="sample_prep",
    )(proj, proj, proj, conv_state, conv_state, conv_state, conv_w, conv_w, conv_w,
      gates, projb, projb, lb_logits)


def _gdn_step_kernel(s_ref, qt_ref, kt_ref, v_ref, eg_ref, beta_ref, pz_ref, nw_ref,
                     so_ref, o_ref, obuf):
    i = pl.program_id(0)
    bt = v_ref.shape[0]
    qt = qt_ref[...]
    kt = kt_ref[...]

    def body(bb, carry):
        bg = i * bt + bb
        kcol = _lane_pick(kt, bg)
        qcol = _lane_pick(qt, bg)
        sd = s_ref[bb] * eg_ref[pl.ds(bb, 1), :]
        ks = jnp.sum(sd * kcol, axis=0, keepdims=True)
        u = beta_ref[pl.ds(bb, 1), :] * (v_ref[pl.ds(bb, 1), :] - ks)
        sn = sd + kcol * u
        so_ref[bb] = sn
        obuf[pl.ds(bb, 1), :] = jnp.sum(sn * qcol, axis=0, keepdims=True)
        return carry

    lax.fori_loop(0, bt, body, 0, unroll=STEP_UNROLL)
    o_ref[...] = _rms_gate(obuf[...], nw_ref[...], pz_ref[...])


def _hgrn_step_kernel(s_ref, qt_ref, ft_ref, pi_ref, pz_ref, nw_ref, so_ref, o_ref, obuf):
    i = pl.program_id(0)
    bt = pi_ref.shape[0]
    qt = qt_ref[...]
    ft = ft_ref[...]

    def body(bb, carry):
        bg = i * bt + bb
        fcol = _lane_pick(ft, bg)
        qcol = _lane_pick(qt, bg)
        sn = fcol * s_ref[bb] + (1.0 - fcol) * pi_ref[pl.ds(bb, 1), :]
        so_ref[bb] = sn
        obuf[pl.ds(bb, 1), :] = jnp.sum(sn * qcol, axis=0, keepdims=True)
        return carry

    lax.fori_loop(0, bt, body, 0, unroll=STEP_UNROLL)
    o_ref[...] = _rms_gate(obuf[...], nw_ref[...], pz_ref[...])


def _sample_step(kind, state, cols_t, rows, projb, norm_w, n_prompt, n_heads):
    n_sample = state.shape[0]
    bt = _pick(n_sample, 16, 8)
    rb = n_prompt // bt
    sspec = pl.BlockSpec((bt, None, HEAD_DIM, HEAD_DIM), lambda i, h: (i, h, 0, 0))
    tspec = pl.BlockSpec((HEAD_DIM, n_sample), lambda i, h: (h, 0))
    rspec = pl.BlockSpec((bt, HEAD_DIM), lambda i, h: (i, h))

    def pspec(seg):
        return pl.BlockSpec((bt, HEAD_DIM), lambda i, h: (rb + i, seg * n_heads + h))

    nspec = pl.BlockSpec((1, HEAD_DIM), lambda i, h: (0, 0))
    if kind == "gdn":
        kern = _gdn_step_kernel
        in_specs = [sspec, tspec, tspec, rspec, rspec, rspec, pspec(0), nspec]
        args = [state, *cols_t, *rows, projb, norm_w]
    else:
        kern = _hgrn_step_kernel
        in_specs = [sspec, tspec, tspec, pspec(3), pspec(4), nspec]
        args = [state, *cols_t, projb, projb, norm_w]
    return pl.pallas_call(
        kern,
        grid=(n_sample // bt, n_heads),
        in_specs=in_specs,
        out_specs=[sspec, rspec],
        out_shape=[jax.ShapeDtypeStruct(state.shape, F32),
                   jax.ShapeDtypeStruct((n_sample, n_heads * HEAD_DIM), F32)],
        scratch_shapes=[pltpu.VMEM((bt, HEAD_DIM), F32)],
        compiler_params=_params(2),
        name=kind + "_step",
    )(*args)


def _out_proj_kernel(ra_ref, rb_ref, oa_ref, oas_ref, ob_ref, obs_ref, x_ref, w_ref, g_ref, b_ref,
                     h_ref, *, alpha, n_prompt_tiles):
    is_prompt = pl.program_id(0) < n_prompt_tiles
    oa = jnp.where(is_prompt, oa_ref[...], oas_ref[...])
    ob = jnp.where(is_prompt, ob_ref[...], obs_ref[...])
    merged = _sigmoid(ra_ref[...]) * oa + _sigmoid(rb_ref[...]) * ob
    mix = _dot(merged.astype(BF16), w_ref[...])
    h_ref[...] = _layer_norm(alpha * x_ref[...] + mix, g_ref[...], b_ref[...])


def _out_projection(projb, oa, oa_s, ob, ob_s, x, w_out, ln_g, ln_b, alpha):
    n, d = x.shape
    n_prompt = oa.shape[0]
    tm = oa_s.shape[0]
    assert n_prompt % tm == 0 and n == n_prompt + tm
    npt = n_prompt // tm
    row = pl.BlockSpec((tm, d), lambda i: (i, 0))
    prow = pl.BlockSpec((tm, d), lambda i: (jnp.minimum(i, npt - 1), 0))
    srow = pl.BlockSpec((tm, d), lambda i: (0, 0))
    vec = pl.BlockSpec((1, d), lambda i: (0, 0))
    return pl.pallas_call(
        functools.partial(_out_proj_kernel, alpha=alpha, n_prompt_tiles=npt),
        grid=(n // tm,),
        in_specs=[pl.BlockSpec((tm, d), lambda i: (i, 5)),
                  pl.BlockSpec((tm, d), lambda i: (i, 6)),
                  prow, srow, prow, srow, row,
                  pl.BlockSpec((d, d), lambda i: (0, 0)), vec, vec],
        out_specs=row,
        out_shape=jax.ShapeDtypeStruct((n, d), F32),
        compiler_params=_params(1),
        name="out_projection",
    )(projb, projb, oa, oa_s, ob, ob_s, x, w_out, ln_g, ln_b)


def _router_kernel(h_ref, w_ref, b_ref, r_ref, cnt_ref, carry, *, n_experts):
    i = pl.program_id(0)

    @pl.when(i == 0)
    def _():
        carry[...] = jnp.zeros_like(carry)

    logits = _dot3(h_ref[...], w_ref[...]) + b_ref[...]
    tm = logits.shape[0]
    lane = _iota(logits.shape, 1)
    x = jnp.where(lane < n_experts, logits, -jnp.inf)
    vals, idxs = [], []
    for _ in range(TOP_K):
        m = jnp.max(x, axis=1, keepdims=True)
        idx = jnp.min(jnp.where(x == m, lane, LANES), axis=1, keepdims=True)
        vals.append(m)
        idxs.append(idx)
        x = jnp.where(lane == idx, -jnp.inf, x)
    es = [jnp.exp(v - vals[0]) for v in vals]
    den = es[0] + es[1] + es[2] + es[3]
    hot = jnp.zeros(logits.shape, F32)
    for idx in idxs:
        hot = hot + jnp.where(lane == idx, 1.0, 0.0)
    r = _iota((tm, tm), 0)
    c = _iota((tm, tm), 1)
    before = _dot(jnp.where(r > c, 1.0, 0.0).astype(BF16), hot.astype(BF16)) + carry[...]
    out = jnp.zeros(logits.shape, F32)
    for k in range(TOP_K):
        rank = jnp.sum(jnp.where(lane == idxs[k], before, 0.0), axis=1, keepdims=True)
        out = out + jnp.where(lane == k, idxs[k].astype(F32), 0.0)
        out = out + jnp.where(lane == TOP_K + k, rank, 0.0)
        out = out + jnp.where(lane == 2 * TOP_K + k, es[k] / den, 0.0)
    r_ref[...] = out
    carry[...] = carry[...] + jnp.sum(hot, axis=0, keepdims=True)
    cnt_ref[...] = carry[...]


def _router(h, w_r, b_r, n_experts):
    n, d = h.shape
    tm = _pick(n, 208, 16)
    return pl.pallas_call(
        functools.partial(_router_kernel, n_experts=n_experts),
        grid=(n // tm,),
        in_specs=[pl.BlockSpec((tm, d), lambda i: (i, 0)),
                  pl.BlockSpec((d, LANES), lambda i: (0, 0)),
                  pl.BlockSpec((1, LANES), lambda i: (0, 0))],
        out_specs=[pl.BlockSpec((tm, LANES), lambda i: (i, 0)),
                   pl.BlockSpec((1, LANES), lambda i: (0, 0))],
        out_shape=[jax.ShapeDtypeStruct((n, LANES), F32),
                   jax.ShapeDtypeStruct((1, LANES), F32)],
        scratch_shapes=[pltpu.VMEM((1, LANES), F32)],
        compiler_params=_params(1),
        name="router",
    )(h, w_r, b_r)


def _dispatch_kernel(dest_ref, h_ref, xin_ref, xb_ref, sem):
    del xin_ref
    i = pl.program_id(0)
    tm = h_ref.shape[0]

    def row_copy(r, k):
        dst = dest_ref[(i * tm + r) * TOP_K + k]
        return pltpu.make_async_copy(h_ref.at[pl.ds(r, 1), :], xb_ref.at[pl.ds(dst, 1), :], sem)

    def start(r, carry):
        for k in range(TOP_K):
            row_copy(r, k).start()
        return carry

    def wait(r, carry):
        for k in range(TOP_K):
            row_copy(r, k).wait()
        return carry

    lax.fori_loop(0, tm, start, 0)
    lax.fori_loop(0, tm, wait, 0)


def _dispatch(dest_flat, h, xb_init):
    n, d = h.shape
    tm = _pick(n, 128, 8)
    return pl.pallas_call(
        _dispatch_kernel,
        grid_spec=pltpu.PrefetchScalarGridSpec(
            num_scalar_prefetch=1,
            grid=(n // tm,),
            in_specs=[pl.BlockSpec((tm, d), lambda i, dest: (i, 0)),
                      pl.BlockSpec(memory_space=pl.ANY)],
            out_specs=pl.BlockSpec(memory_space=pl.ANY),
            scratch_shapes=[pltpu.SemaphoreType.DMA(())]),
        out_shape=jax.ShapeDtypeStruct(xb_init.shape, F32),
        input_output_aliases={2: 0},
        compiler_params=_params(1),
        name="dispatch",
    )(dest_flat, h, xb_init)


def _swiglu_interleaved(h):
    m = h.shape[0]
    lane = _iota((m, LANES), 1)
    low = lane < LANES // 2
    evens_first = jnp.where(low, 2 * lane, 2 * lane - (LANES - 1))
    acts = []
    for p in range(h.shape[1] // (2 * LANES)):
        a = jnp.take_along_axis(h[:, 2 * p * LANES:(2 * p + 1) * LANES], evens_first, axis=1)
        b = jnp.take_along_axis(h[:, (2 * p + 1) * LANES:(2 * p + 2) * LANES], evens_first, axis=1)
        gate = jnp.where(low, a, pltpu.roll(b, LANES // 2, 1))
        up = jnp.where(low, pltpu.roll(a, LANES // 2, 1), b)
        gate = jnp.minimum(gate, SWIGLU_LIMIT)
        up = jnp.clip(up, -SWIGLU_LIMIT, SWIGLU_LIMIT)
        acts.append((up + 1.0) * (gate * _sigmoid(gate * SWIGLU_ALPHA)))
    return jnp.concatenate(acts, axis=1)


def _moe_kernel(tile_ref, exp_ref, blk0_ref, nblk_ref, zero_ref, x_ref, wgu_ref, bgu_ref, wdn_ref,
                bdn_ref, y_ref):
    del tile_ref, exp_ref
    it = pl.program_id(0)
    j = pl.program_id(1)
    blk0 = blk0_ref[it]
    nblk = nblk_ref[it]

    @pl.when(jnp.logical_and(j == 0, zero_ref[it] == 1))
    def _():
        y_ref[...] = jnp.zeros_like(y_ref)

    def rows_step(blk, n_rows):
        r0 = pl.multiple_of(blk * MOE_ROW_BLOCK, MOE_ROW_BLOCK)
        xs = x_ref[pl.ds(r0, n_rows), :].astype(BF16)
        h = _dot(xs, wgu_ref[...].astype(BF16)) + bgu_ref[...]
        contrib = _dot(_swiglu_interleaved(h).astype(BF16), wdn_ref[...].astype(BF16))
        prev = jnp.where(j == 0, jnp.broadcast_to(bdn_ref[...], contrib.shape),
                         y_ref[pl.ds(r0, n_rows), :])
        y_ref[pl.ds(r0, n_rows), :] = prev + contrib

    def pair(bi, carry):
        rows_step(blk0 + 2 * bi, 2 * MOE_ROW_BLOCK)
        return carry

    lax.fori_loop(0, nblk // 2, pair, 0)

    @pl.when(nblk % 2 == 1)
    def _():
        rows_step(blk0 + nblk - 1, MOE_ROW_BLOCK)


def _moe_gemm(item_tile, item_exp, item_blk0, item_nblk, item_zero, xb, w_gu, b_gu, w_dn, b_dn):
    p_rows = xb.shape[0]
    n_exp, d, two_de = w_gu.shape
    de = two_de // 2
    n_items = item_tile.shape[0]
    nj = two_de // MOE_COL_TILE
    dn_rows = MOE_COL_TILE // 2

    def live_j(it, j, nblk):
        return jnp.where(nblk[it] > 0, j, nj - 1)

    return pl.pallas_call(
        _moe_kernel,
        grid_spec=pltpu.PrefetchScalarGridSpec(
            num_scalar_prefetch=5,
            grid=(n_items, nj),
            in_specs=[
                pl.BlockSpec((MOE_ROW_TILE, d), lambda it, j, tl, ex, b0, nb, zf: (tl[it], 0)),
                pl.BlockSpec((None, d, MOE_COL_TILE),
                             lambda it, j, tl, ex, b0, nb, zf: (ex[it], 0, live_j(it, j, nb))),
                pl.BlockSpec((None, 1, MOE_COL_TILE),
                             lambda it, j, tl, ex, b0, nb, zf: (ex[it], 0, live_j(it, j, nb))),
                pl.BlockSpec((None, dn_rows, d),
                             lambda it, j, tl, ex, b0, nb, zf: (ex[it], live_j(it, j, nb), 0)),
                pl.BlockSpec((None, 1, d), lambda it, j, tl, ex, b0, nb, zf: (ex[it], 0, 0)),
            ],
            out_specs=pl.BlockSpec((MOE_ROW_TILE, d), lambda it, j, tl, ex, b0, nb, zf: (tl[it], 0))),
        out_shape=jax.ShapeDtypeStruct((p_rows, d), F32),
        compiler_params=_params(2),
        name="moe_gemm",
    )(item_tile, item_exp, item_blk0, item_nblk, item_zero, xb, w_gu, b_gu.reshape(n_exp, 1, two_de),
      w_dn, b_dn.reshape(n_exp, 1, d))


def _combine_kernel(dest_ref, h_ref, r_ref, g_ref, b_ref, yb_ref, o_ref, rows, sem, *, alpha):
    i = pl.program_id(0)
    tm, d = h_ref.shape

    def row_copy(r, k):
        src = dest_ref[(i * tm + r) * TOP_K + k]
        return pltpu.make_async_copy(yb_ref.at[pl.ds(src, 1), :], rows.at[k, pl.ds(r, 1), :], sem)

    def start(r, carry):
        for k in range(TOP_K):
            row_copy(r, k).start()
        return carry

    def wait(r, carry):
        for k in range(TOP_K):
            row_copy(r, k).wait()
        return carry

    lax.fori_loop(0, tm, start, 0)
    lax.fori_loop(0, tm, wait, 0)
    rr = r_ref[...]
    ffn = jnp.zeros((tm, d), F32)
    for k in range(TOP_K):
        ffn = ffn + _lane_pick(rr, 2 * TOP_K + k) * rows[k]
    o_ref[...] = _layer_norm(alpha * h_ref[...] + ffn, g_ref[...], b_ref[...])


def _combine(dest_flat, h, route, ln_g, ln_b, yb, alpha):
    n, d = h.shape
    tm = _pick(n, 128, 8)
    return pl.pallas_call(
        functools.partial(_combine_kernel, alpha=alpha),
        grid_spec=pltpu.PrefetchScalarGridSpec(
            num_scalar_prefetch=1,
            grid=(n // tm,),
            in_specs=[pl.BlockSpec((tm, d), lambda i, dest: (i, 0)),
                      pl.BlockSpec((tm, LANES), lambda i, dest: (i, 0)),
                      pl.BlockSpec((1, d), lambda i, dest: (0, 0)),
                      pl.BlockSpec((1, d), lambda i, dest: (0, 0)),
                      pl.BlockSpec(memory_space=pl.ANY)],
            out_specs=pl.BlockSpec((tm, d), lambda i, dest: (i, 0)),
            scratch_shapes=[pltpu.VMEM((TOP_K, tm, d), F32),
                            pltpu.SemaphoreType.DMA(())]),
        out_shape=jax.ShapeDtypeStruct((n, d), F32),
        compiler_params=_params(1),
        name="combine",
    )(dest_flat, h, route, ln_g, ln_b, yb)


def _routing_tables(route, counts_row, n_experts):
    n = route.shape[0]
    e_idx = route[:, 0:TOP_K].astype(I32)
    rank = route[:, TOP_K:2 * TOP_K].astype(I32)
    counts = counts_row[0, :n_experts].astype(I32)
    pcounts = (counts + MOE_ROW_BLOCK - 1) // MOE_ROW_BLOCK * MOE_ROW_BLOCK
    pends = jnp.cumsum(pcounts)
    pstarts = pends - pcounts
    dest = (pstarts[e_idx] + rank).reshape(-1)

    p_rows = -(-(n * TOP_K + n_experts * MOE_ROW_BLOCK) // MOE_ROW_TILE) * MOE_ROW_TILE
    n_blocks = p_rows // MOE_ROW_BLOCK
    bpt = MOE_ROW_TILE // MOE_ROW_BLOCK
    max_items = p_rows // MOE_ROW_TILE + n_experts
    blk = jnp.arange(n_blocks, dtype=I32)
    valid = blk * MOE_ROW_BLOCK < pends[-1]
    blk_e = jnp.minimum(jnp.sum((pends[None, :] <= (blk * MOE_ROW_BLOCK)[:, None]).astype(I32), axis=1),
                        n_experts - 1)
    first = valid & ((blk % bpt == 0) | (blk_e != jnp.roll(blk_e, 1)))
    item_of_blk = jnp.cumsum(first.astype(I32)) - 1
    n_items = jnp.sum(first.astype(I32))
    it = jnp.arange(max_items, dtype=I32)
    mine = item_of_blk[None, :] == it[:, None]
    item_first = jnp.sum(jnp.where(mine & first[None, :], blk[None, :], 0), axis=1)
    item_nblk = jnp.sum((mine & valid[None, :]).astype(I32), axis=1)
    live = it < n_items
    item_first = item_first[jnp.minimum(it, n_items - 1)]
    item_nblk = jnp.where(live, item_nblk, 0)
    item_blk0 = item_first % bpt
    n_tiles = p_rows // MOE_ROW_TILE
    last_tile = item_first[max_items - 1] // bpt
    idle_tile = last_tile + 1 + (it - n_items)
    item_tile = jnp.where(live, item_first // bpt, jnp.minimum(idle_tile, n_tiles - 1))
    item_zero = jnp.where(live, (item_blk0 == 0) & (item_nblk > 0), idle_tile < n_tiles)
    return (dest, p_rows, item_tile, blk_e[item_first], item_blk0, item_nblk,
            item_zero.astype(I32))


def kernel(x_prompt, x_sample, state_gdn_conv, state_gdn_s, state_hgrn_s, w_in, gdn_conv_w,
           gdn_a_log, gdn_dt_bias, gdn_norm_w, hg_lb_logits, hg_norm_w, w_out, ln1_g, ln1_b,
           w_router, b_router, w_gate_up, b_gate_up, w_down, b_down, ln2_g, ln2_b):
    depth = w_in.shape[0]
    assert depth == 1
    batch, seq, d = x_prompt.shape
    n_sample = x_sample.shape[0]
    assert x_sample.shape[1] == 1
    n_heads = d // HEAD_DIM
    n_prompt = batch * seq
    n_rows = n_prompt + n_sample
    n_experts = w_router.shape[-1]
    alpha = (2.0 * depth) ** 0.25
    qkv = 3 * d
    ab0 = qkv
    ab1 = qkv + 2 * n_heads

    x_all = jnp.concatenate([x_prompt.reshape(n_prompt, d), x_sample.reshape(n_sample, d)],
                            axis=0).astype(F32)
    wt = jnp.swapaxes(w_in[0], 0, 1).astype(F32)
    w_ab = jnp.pad(wt[ab0:ab1], ((0, LANES - 2 * n_heads), (0, 0)))
    prm = jnp.zeros((8, LANES), F32)
    prm = prm.at[0, :n_heads].set(gdn_a_log[0].astype(F32))
    prm = prm.at[1, :n_heads].set(gdn_dt_bias[0].astype(F32))
    conv_w = gdn_conv_w[0].astype(F32)
    gdn_nw = gdn_norm_w[0].astype(F32).reshape(1, HEAD_DIM)
    hg_nw = hg_norm_w[0].astype(F32).reshape(1, HEAD_DIM)
    lb_logits = hg_lb_logits.astype(F32)

    proj = _in_projection(x_all, wt, 0, qkv, 512)
    projb = _in_projection(x_all, wt, ab1, wt.shape[0] - ab1, 512)
    gates, gates_t = _gates(x_all, w_ab, prm, n_prompt, n_heads)
    oa, gdn_s_prompt = _gdn_prompt(proj, projb, gates, gates_t, conv_w, gdn_nw, batch, seq, n_heads)
    ob, hg_s_prompt = _hgrn_prompt(projb, lb_logits, hg_nw, batch, seq, n_heads)

    conv_state = state_gdn_conv[0].astype(F32)
    sq, sk, sv, seg, sbeta, sf, sqh = _sample_prep(proj, projb, conv_state, conv_w, gates,
                                                   lb_logits,
                                                   n_prompt, n_sample, n_heads)
    gdn_s_sample, oa_s = _sample_step("gdn", state_gdn_s[0].astype(F32), (sq.T, sk.T),
                                      (sv, seg, sbeta), projb, gdn_nw, n_prompt, n_heads)
    hg_s_sample, ob_s = _sample_step("hgrn", state_hgrn_s[0].astype(F32), (sqh.T, sf.T), (),
                                     projb, hg_nw, n_prompt, n_heads)

    h = _out_projection(projb, oa, oa_s, ob, ob_s, x_all, w_out[0].astype(BF16),
                        ln1_g[0].astype(F32).reshape(1, d), ln1_b[0].astype(F32).reshape(1, d),
                        alpha)

    w_r = jnp.pad(w_router[0].astype(F32), ((0, 0), (0, LANES - n_experts)))
    b_r = jnp.pad(b_router[0].astype(F32), (0, LANES - n_experts)).reshape(1, LANES)
    route, counts_row = _router(h, w_r, b_r, n_experts)
    dest, p_rows, item_tile, item_exp, item_blk0, item_nblk, item_zero = _routing_tables(
        route, counts_row, n_experts)

    xb = _dispatch(dest, h, jnp.zeros((p_rows, d), F32))
    yb = _moe_gemm(item_tile, item_exp, item_blk0, item_nblk, item_zero, xb, w_gate_up[0],
                   b_gate_up[0], w_down[0], b_down[0])
    y = _combine(dest, h, route, ln2_g[0].astype(F32).reshape(1, d),
                 ln2_b[0].astype(F32).reshape(1, d), yb, alpha)

    y_prompt = y[:n_prompt].reshape(batch, seq, d).astype(x_prompt.dtype)
    y_sample = y[n_prompt:].reshape(n_sample, 1, d).astype(x_sample.dtype)
    new_conv_p = jnp.stack([proj[(b + 1) * seq - 3:(b + 1) * seq] for b in range(batch)])[None]
    new_conv_s = jnp.concatenate([conv_state[:, 1:, :], proj[n_prompt:, None, :]], axis=1)[None]
    sdt = state_gdn_s.dtype
    return (y_prompt, y_sample,
            new_conv_p.astype(state_gdn_conv.dtype), gdn_s_prompt[None].astype(sdt),
            hg_s_prompt[None].astype(state_hgrn_s.dtype),
            new_conv_s.astype(state_gdn_conv.dtype), gdn_s_sample[None].astype(sdt),
            hg_s_sample[None].astype(state_hgrn_s.dtype))
```

```python
import functools

import jax
import jax.numpy as jnp
from jax import lax
from jax.experimental import pallas as pl
from jax.experimental.pallas import tpu as pltpu

F32 = jnp.float32
BF16 = jnp.bfloat16
I32 = jnp.int32
U32 = jnp.uint32

HEAD_DIM = 128
LANES = 128
GDN_CHUNK = 64
HG_SUB = 16
TIME_BLOCK = 256
GDN_HEADS_PER_STEP = 4
HG_HEADS_PER_STEP = 2
STEP_UNROLL = 4
TOP_K = 4
MOE_ROW_BLOCK = 256
MOE_ROW_TILE = 1024
MOE_COL_TILE = 512
SWIGLU_LIMIT = 7.0
SWIGLU_ALPHA = 1.702
LN_EPS = 1e-5
RMS_EPS = 1e-6
L2_EPS = 1e-6
VMEM_LIMIT = 56 * 1024 * 1024


def _params(n_axes, vmem=VMEM_LIMIT):
    return pltpu.CompilerParams(dimension_semantics=("arbitrary",) * n_axes,
                                vmem_limit_bytes=vmem)


def _pick(n, target, mult):
    best = None
    for d in range(mult, min(n, target) + 1, mult):
        if n % d == 0:
            best = d
    assert best is not None, (n, target, mult)
    return best


def _dot(a, b):
    return jnp.dot(a, b, preferred_element_type=F32)


def _dot_nt(a, b):
    return lax.dot_general(a, b, (((1,), (1,)), ((), ())), preferred_element_type=F32)


def _dot_tn(a, b):
    return lax.dot_general(a, b, (((0,), (0,)), ((), ())), preferred_element_type=F32)


def _hi_lo(x):
    hi = x.astype(BF16)
    lo = (x - hi.astype(F32)).astype(BF16)
    return hi, lo


def _dot3(a, b):
    ah, al = _hi_lo(a)
    bh, bl = _hi_lo(b)
    return _dot(ah, bh) + (_dot(ah, bl) + _dot(al, bh))


def _dot_exact_lhs(m_bf16, x):
    p1 = x.astype(BF16)
    r1 = x - p1.astype(F32)
    p2 = r1.astype(BF16)
    p3 = (r1 - p2.astype(F32)).astype(BF16)
    return _dot(m_bf16, p1) + (_dot(m_bf16, p2) + _dot(m_bf16, p3))


def _sigmoid(x):
    return 1.0 / (1.0 + jnp.exp(-x))


def _silu(x):
    return x * _sigmoid(x)


def _softplus(x):
    return jnp.maximum(x, 0.0) + jnp.log(1.0 + jnp.exp(-jnp.abs(x)))


def _iota(shape, dim):
    return lax.broadcasted_iota(I32, shape, dim)


def _layer_norm(y, g, b):
    mu = jnp.mean(y, axis=-1, keepdims=True)
    yc = y - mu
    var = jnp.mean(yc * yc, axis=-1, keepdims=True)
    return yc * lax.rsqrt(var + LN_EPS) * g + b


def _rms_gate(o, w, z):
    return o * lax.rsqrt(jnp.mean(o * o, axis=-1, keepdims=True) + RMS_EPS) * w * _silu(z)


def _l2norm(x):
    return x * lax.rsqrt(jnp.sum(x * x, axis=-1, keepdims=True) + L2_EPS)


def _lane_pick(x, idx):
    lane = _iota(x.shape, 1)
    return jnp.sum(jnp.where(lane == idx, x, 0.0), axis=1, keepdims=True)


def _mm_nt_kernel(x_ref, wt_ref, o_ref):
    o_ref[...] = _dot_nt(x_ref[...].astype(BF16), wt_ref[...].astype(BF16))


def _in_projection(x, wt, row0, no, tn_target):
    n, d = x.shape
    tm = _pick(n, 1040, 16)
    tn = _pick(no, tn_target, LANES)
    if row0 % tn == 0:
        wspec = pl.BlockSpec((tn, d), lambda i, j: (row0 // tn + j, 0))
    else:
        assert row0 % 8 == 0
        wspec = pl.BlockSpec((pl.Element(tn), pl.Element(d)),
                             lambda i, j: (pl.multiple_of(row0 + j * tn, 8), 0))
    return pl.pallas_call(
        _mm_nt_kernel,
        grid=(n // tm, no // tn),
        in_specs=[pl.BlockSpec((tm, d), lambda i, j: (i, 0)), wspec],
        out_specs=pl.BlockSpec((tm, tn), lambda i, j: (i, j)),
        out_shape=jax.ShapeDtypeStruct((n, no), F32),
        compiler_params=_params(2),
        name="in_projection",
    )(x, wt)


def _gates_kernel(x_ref, w_ref, prm_ref, g_ref, gt_ref, *, n_prompt_tiles, n_heads):
    i = pl.program_id(0)
    xh, xl = _hi_lo(x_ref[...])
    wh, wl = _hi_lo(w_ref[...])
    ab = _dot_nt(xh, wh) + (_dot_nt(xh, wl) + _dot_nt(xl, wh))
    prm = prm_ref[...]
    g = -jnp.exp(prm[0:1]) * _softplus(ab + prm[1:2])
    beta = _sigmoid(ab)
    tm = ab.shape[0]
    r = _iota((tm, tm), 0)
    c = _iota((tm, tm), 1)
    shift = jnp.where(i < n_prompt_tiles, GDN_CHUNK.bit_length() - 1, 0)
    tri = jnp.logical_and((r >> shift) == (c >> shift), r >= c)
    gc = _dot_exact_lhs(jnp.where(tri, 1.0, 0.0).astype(BF16), g)
    lane = _iota(ab.shape, 1)
    out = jnp.where(lane < n_heads, gc, beta)
    g_ref[...] = out
    gt_ref[...] = out.T


def _gates(x, w_ab, prm, n_prompt, n_heads):
    n, d = x.shape
    tm = LANES
    assert n % tm == 0 and n_prompt % tm == 0 and tm % GDN_CHUNK == 0
    kern = functools.partial(_gates_kernel, n_prompt_tiles=n_prompt // tm, n_heads=n_heads)
    return pl.pallas_call(
        kern,
        grid=(n // tm,),
        in_specs=[pl.BlockSpec((tm, d), lambda i: (i, 0)),
                  pl.BlockSpec((LANES, d), lambda i: (0, 0)),
                  pl.BlockSpec((8, LANES), lambda i: (0, 0))],
        out_specs=[pl.BlockSpec((tm, LANES), lambda i: (i, 0)),
                   pl.BlockSpec((LANES, tm), lambda i: (0, i))],
        out_shape=[jax.ShapeDtypeStruct((n, LANES), F32),
                   jax.ShapeDtypeStruct((LANES, n), F32)],
        compiler_params=_params(1),
        name="gates",
    )(x, w_ab, prm)


def _split_dot3(ah, al, bh, bl):
    return _dot(ah, bh) + (_dot(ah, bl) + _dot(al, bh))


def _unit_lower_inverses(mats):
    c = mats[0].shape[0]
    r = _iota((c, c), 0)
    col = _iota((c, c), 1)
    eye = jnp.where(r == col, 1.0, 0.0)
    pair = jnp.logical_and((r >> 1) == (col >> 1), r > col)
    invs = [eye - jnp.where(pair, a, 0.0) for a in mats]
    level = 2
    while (1 << level) <= c:
        half = level - 1
        mask = jnp.logical_and(
            (r >> level) == (col >> level),
            jnp.logical_and(((r >> half) & 1) == 1, ((col >> half) & 1) == 0))
        lows = [_hi_lo(jnp.where(mask, a, 0.0)) for a in mats]
        inv_s = [_hi_lo(inv) for inv in invs]
        xs = [_split_dot3(lh, ll, ih, il) for (lh, ll), (ih, il) in zip(lows, inv_s)]
        x_s = [_hi_lo(x) for x in xs]
        invs = [inv - _split_dot3(ih, il, xh, xl)
                for inv, (ih, il), (xh, xl) in zip(invs, inv_s, x_s)]
        level += 1
    return invs


def _gdn_prompt_kernel(pq_ref, pk_ref, pv_ref, pz_ref, g_ref, gt_ref, wq_ref, wk_ref, wv_ref,
                       nw_ref, o_ref, s_ref, state, cbuf, qs, ks, vs):
    hg = pl.program_id(1)
    t = pl.program_id(2)
    tb = pq_ref.shape[0]
    hb = state.shape[0]
    n_heads = pl.num_programs(1) * hb
    cl = GDN_CHUNK

    @pl.when(t == 0)
    def _():
        state[...] = jnp.zeros_like(state)
        cbuf[:, 0:8, :] = jnp.zeros((3, 8, hb * HEAD_DIM), F32)

    def conv(idx, u_ref, w_ref):
        u = u_ref[...]
        cbuf[idx, 8:8 + tb, :] = u
        w = w_ref[...]
        y = cbuf[idx, 5:5 + tb, :] * w[0:1, :]
        for j in range(1, 4):
            y = y + cbuf[idx, 5 + j:5 + j + tb, :] * w[j:j + 1, :]
        cbuf[idx, 0:8, :] = u[tb - 8:tb, :]
        return _silu(y)

    qc = conv(0, pq_ref, wq_ref)
    kc = conv(1, pk_ref, wk_ref)
    vs[...] = conv(2, pv_ref, wv_ref)
    for i in range(hb):
        sl = slice(i * HEAD_DIM, (i + 1) * HEAD_DIM)
        qs[:, sl] = _l2norm(qc[:, sl]) * (HEAD_DIM ** -0.5)
        ks[:, sl] = _l2norm(kc[:, sl])

    gall = g_ref[...]
    gt = gt_ref[...]
    gc_cols = [_lane_pick(gall, hg * hb + i) for i in range(hb)]
    beta_cols = [_lane_pick(gall, n_heads + hg * hb + i) for i in range(hb)]

    r = _iota((cl, cl), 0)
    col = _iota((cl, cl), 1)
    incl = r >= col
    strict = r > col

    pairs = [(c, i) for c in range(tb // cl) for i in range(hb)]
    pre = []
    for c, i in pairs:
        rows = slice(c * cl, (c + 1) * cl)
        sl = slice(i * HEAD_DIM, (i + 1) * HEAD_DIM)
        q = qs[rows, sl]
        k = ks[rows, sl]
        v = vs[rows, sl]
        gcc = gc_cols[i][rows, :]
        gcr = gt[i:i + 1, rows]
        bc = beta_cols[i][rows, :]
        decay = jnp.where(incl, jnp.exp(jnp.where(incl, gcc - gcr, 0.0)), 0.0)
        kb = k.astype(BF16)
        a = jnp.where(strict, bc * decay * _dot_nt(kb, kb), 0.0)
        aqk = (_dot_nt(q.astype(BF16), kb) * decay).astype(BF16)
        egc = jnp.exp(gcc)
        g_last = gcc[cl - 1:cl, :]
        pre.append(dict(a=a, aqk=aqk, rhs=_hi_lo(jnp.concatenate([bc * egc * k, bc * v], axis=1)),
                        qd=(q * egc).astype(BF16), ke=(k * jnp.exp(g_last - gcc)).astype(BF16),
                        g_end=jnp.exp(g_last)))
    invs = _unit_lower_inverses([p["a"] for p in pre])
    for p, inv in zip(pre, invs):
        ih, il = _hi_lo(inv)
        wu = _split_dot3(ih, il, *p["rhs"])
        p["w"] = wu[:, :HEAD_DIM].astype(BF16)
        p["u0"] = wu[:, HEAD_DIM:]

    s = [state[i] for i in range(hb)]
    outs = [[] for _ in range(hb)]
    for c in range(tb // cl):
        ps = [pre[c * hb + i] for i in range(hb)]
        sb = [x.astype(BF16) for x in s]
        ws = [_dot(p["w"], b) for p, b in zip(ps, sb)]
        qsd = [_dot(p["qd"], b) for p, b in zip(ps, sb)]
        ub = [(p["u0"] - x).astype(BF16) for p, x in zip(ps, ws)]
        s = [p["g_end"] * x + _dot_tn(p["ke"], u) for p, x, u in zip(ps, s, ub)]
        for i in range(hb):
            outs[i].append(qsd[i] + _dot(ps[i]["aqk"], ub[i]))
    nw = nw_ref[...]
    for i in range(hb):
        sl = slice(i * HEAD_DIM, (i + 1) * HEAD_DIM)
        state[i] = s[i]
        o_ref[:, sl] = _rms_gate(jnp.concatenate(outs[i], axis=0), nw, pz_ref[:, sl])

    @pl.when(t == pl.num_programs(2) - 1)
    def _():
        for i in range(hb):
            s_ref[i] = s[i]


def _gdn_prompt(proj, projb, gates, gates_t, conv_w, norm_w, batch, seq, n_heads):
    tb = _pick(seq, TIME_BLOCK, GDN_CHUNK)
    nt = seq // tb
    hb = _pick(n_heads, GDN_HEADS_PER_STEP, 1)
    ng = n_heads // hb
    d = n_heads * HEAD_DIM
    n_rows = gates_t.shape[1]
    gates_t3 = gates_t[:n_heads].reshape(ng, hb, n_rows)

    def rows(b, h, t):
        return b * nt + t

    def pspec(seg):
        return pl.BlockSpec((tb, hb * HEAD_DIM), lambda b, h, t: (rows(b, h, t), seg * ng + h))

    def wspec(seg):
        return pl.BlockSpec((4, hb * HEAD_DIM), lambda b, h, t: (0, seg * ng + h))

    return pl.pallas_call(
        _gdn_prompt_kernel,
        grid=(batch, ng, nt),
        in_specs=[pspec(0), pspec(1), pspec(2), pspec(0),
                  pl.BlockSpec((tb, LANES), lambda b, h, t: (rows(b, h, t), 0)),
                  pl.BlockSpec((None, hb, tb), lambda b, h, t: (h, 0, rows(b, h, t))),
                  wspec(0), wspec(1), wspec(2),
                  pl.BlockSpec((1, HEAD_DIM), lambda b, h, t: (0, 0))],
        out_specs=[pl.BlockSpec((tb, hb * HEAD_DIM), lambda b, h, t: (rows(b, h, t), h)),
                   pl.BlockSpec((None, hb, HEAD_DIM, HEAD_DIM), lambda b, h, t: (b, h, 0, 0))],
        out_shape=[jax.ShapeDtypeStruct((batch * seq, d), F32),
                   jax.ShapeDtypeStruct((batch, n_heads, HEAD_DIM, HEAD_DIM), F32)],
        scratch_shapes=[pltpu.VMEM((hb, HEAD_DIM, HEAD_DIM), F32),
                        pltpu.VMEM((3, tb + 8, hb * HEAD_DIM), F32),
                        pltpu.VMEM((tb, hb * HEAD_DIM), F32),
                        pltpu.VMEM((tb, hb * HEAD_DIM), F32),
                        pltpu.VMEM((tb, hb * HEAD_DIM), F32)],
        compiler_params=_params(3),
        name="gdn_prompt",
    )(proj, proj, proj, projb, gates, gates_t3, conv_w, conv_w, conv_w, norm_w)


def _lower_bound(logits):
    m = jnp.max(logits, axis=0, keepdims=True)
    e = jnp.exp(logits - m)
    return e[0:1, :] / jnp.sum(e, axis=0, keepdims=True)


def _hgrn_prompt_kernel(pq_ref, pf_ref, pi_ref, pz_ref, lb_ref, nw_ref, o_ref, s_ref,
                        state_t):
    t = pl.program_id(2)
    tb = pq_ref.shape[0]
    hb = state_t.shape[0]

    @pl.when(t == 0)
    def _():
        state_t[...] = jnp.zeros_like(state_t)

    lb = _lower_bound(lb_ref[...])
    f = lb + (1.0 - lb) * _sigmoid(pf_ref[...])
    kk = 1.0 - f
    lf = jnp.log2(f)
    q = _silu(pq_ref[...]) * (HEAD_DIM ** -0.5)
    v = pi_ref[...]

    r = _iota((tb, tb), 0)
    c = _iota((tb, tb), 1)
    sub_shift = HG_SUB.bit_length() - 1
    tri = jnp.logical_and((r >> sub_shift) == (c >> sub_shift), r >= c)
    b = _dot_exact_lhs(jnp.where(tri, 1.0, 0.0).astype(BF16), lf)

    width = hb * HEAD_DIM
    heads = [slice(i * HEAD_DIM, (i + 1) * HEAD_DIM) for i in range(hb)]

    half = HG_SUB // 2
    n_sub = tb // HG_SUB

    def halves(x):
        x4 = x.reshape(n_sub, 2, half, width)
        return x4[:, 0], x4[:, 1]

    q_lo, q_hi = halves(q)
    b_lo, b_hi = halves(b)
    k_lo, k_hi = halves(kk)
    v_lo, v_hi = halves(v)
    row = _iota((1, half, 1), 1)

    def rot(x, d):
        return x if d == 0 else pltpu.roll(x, d, 1)

    def add_terms(acc, qx, bx, kp, bp, vp, ok):
        prod = qx * kp * jnp.exp2(bx - bp)
        for i, sl in enumerate(heads):
            w = jnp.sum(prod[:, :, sl], axis=2, keepdims=True)
            if ok is not None:
                w = jnp.where(ok, w, 0.0)
            acc[i] = acc[i] + w * vp[:, :, sl]

    acc_lo = [jnp.zeros((n_sub, half, HEAD_DIM), F32) for _ in range(hb)]
    acc_hi = [jnp.zeros((n_sub, half, HEAD_DIM), F32) for _ in range(hb)]
    for d in range(half):
        ok = None if d == 0 else row >= d
        kl, bl_, vl = rot(k_lo, d), rot(b_lo, d), rot(v_lo, d)
        add_terms(acc_lo, q_lo, b_lo, kl, bl_, vl, ok)
        add_terms(acc_hi, q_hi, b_hi, kl, bl_, vl, ok)
        if d == 0:
            add_terms(acc_hi, q_hi, b_hi, k_hi, b_hi, v_hi, None)
        else:
            add_terms(acc_hi, q_hi, b_hi, jnp.where(ok, rot(k_hi, d), kl),
                      jnp.where(ok, rot(b_hi, d), bl_), jnp.where(ok, rot(v_hi, d), vl), None)
    o_intra = [jnp.stack([lo_, hi_], axis=1).reshape(tb, HEAD_DIM)
               for lo_, hi_ in zip(acc_lo, acc_hi)]

    n_sub = tb // HG_SUB
    eb = jnp.exp2(b)
    qe = (q * eb).astype(BF16)
    incs, scales = [], []
    for j in range(n_sub):
        rows = slice(j * HG_SUB, (j + 1) * HG_SUB)
        bl = b[(j + 1) * HG_SUB - 1:(j + 1) * HG_SUB, :]
        ke = (kk[rows, :] * jnp.exp2(bl - b[rows, :])).astype(BF16)
        vb = v[rows, :].astype(BF16)
        incs.append([_dot_tn(vb[:, sl], ke[:, sl]) for sl in heads])
        scales.append(eb[(j + 1) * HG_SUB - 1:(j + 1) * HG_SUB, :])
    st = [state_t[i] for i in range(hb)]
    before = []
    for j in range(n_sub):
        before.append([x.astype(BF16) for x in st])
        st = [x * scales[j][:, sl] + inc for x, sl, inc in zip(st, heads, incs[j])]
    nw = nw_ref[...]
    for i, sl in enumerate(heads):
        outs = [_dot_nt(qe[j * HG_SUB:(j + 1) * HG_SUB, sl], before[j][i]) for j in range(n_sub)]
        state_t[i] = st[i]
        o_ref[:, sl] = _rms_gate(o_intra[i] + jnp.concatenate(outs, axis=0), nw, pz_ref[:, sl])

    @pl.when(t == pl.num_programs(2) - 1)
    def _():
        for i in range(hb):
            s_ref[i] = st[i].T


def _hgrn_prompt(projb, lb_logits, norm_w, batch, seq, n_heads):
    tb = _pick(seq, TIME_BLOCK, HG_SUB)
    nt = seq // tb
    hb = _pick(n_heads, HG_HEADS_PER_STEP, 1)
    ng = n_heads // hb
    d = n_heads * HEAD_DIM
    n_lb = lb_logits.shape[0]
    width = hb * HEAD_DIM

    def pspec(seg):
        return pl.BlockSpec((tb, width), lambda b, h, t: (b * nt + t, seg * ng + h))

    return pl.pallas_call(
        _hgrn_prompt_kernel,
        grid=(batch, ng, nt),
        in_specs=[pspec(1), pspec(2), pspec(3), pspec(4),
                  pl.BlockSpec((n_lb, width), lambda b, h, t: (0, h)),
                  pl.BlockSpec((1, HEAD_DIM), lambda b, h, t: (0, 0))],
        out_specs=[pl.BlockSpec((tb, width), lambda b, h, t: (b * nt + t, h)),
                   pl.BlockSpec((None, hb, HEAD_DIM, HEAD_DIM), lambda b, h, t: (b, h, 0, 0))],
        out_shape=[jax.ShapeDtypeStruct((batch * seq, d), F32),
                   jax.ShapeDtypeStruct((batch, n_heads, HEAD_DIM, HEAD_DIM), F32)],
        scratch_shapes=[pltpu.VMEM((hb, HEAD_DIM, HEAD_DIM), F32)],
        compiler_params=_params(3),
        name="hgrn_prompt",
    )(projb, projb, projb, projb, lb_logits, norm_w)


def _sample_prep_kernel(pq_ref, pk_ref, pv_ref, cq_ref, ck_ref, cv_ref, wq_ref, wk_ref, wv_ref,
                        g_ref, hq_ref, hf_ref, lb_ref,
                        q_ref, k_ref, v_ref, eg_ref, beta_ref, f_ref, qh_ref):
    h = pl.program_id(0)

    def conv(u_ref, c_ref, w_ref):
        w = w_ref[...]
        y = u_ref[...] * w[3:4, :]
        for j in range(3):
            y = y + c_ref[:, j, :] * w[j:j + 1, :]
        return _silu(y)

    q_ref[...] = _l2norm(conv(pq_ref, cq_ref, wq_ref)) * (HEAD_DIM ** -0.5)
    k_ref[...] = _l2norm(conv(pk_ref, ck_ref, wk_ref))
    v_ref[...] = conv(pv_ref, cv_ref, wv_ref)
    gall = g_ref[...]
    shape = q_ref.shape
    eg_ref[...] = jnp.broadcast_to(jnp.exp(_lane_pick(gall, h)), shape)
    beta_ref[...] = jnp.broadcast_to(_lane_pick(gall, h + pl.num_programs(0)), shape)
    lb = _lower_bound(lb_ref[...])
    f_ref[...] = lb + (1.0 - lb) * _sigmoid(hf_ref[...])
    qh_ref[...] = _silu(hq_ref[...]) * (HEAD_DIM ** -0.5)


def _sample_prep(proj, projb, conv_state, conv_w, gates, lb_logits, n_prompt, n_sample, n_heads):
    assert n_prompt % n_sample == 0
    rb = n_prompt // n_sample
    d = n_heads * HEAD_DIM
    n_lb = lb_logits.shape[0]

    def pspec(seg):
        return pl.BlockSpec((n_sample, HEAD_DIM), lambda h: (rb, seg * n_heads + h))

    def cspec(seg):
        return pl.BlockSpec((n_sample, 3, HEAD_DIM), lambda h: (0, 0, seg * n_heads + h))

    def wspec(seg):
        return pl.BlockSpec((4, HEAD_DIM), lambda h: (0, seg * n_heads + h))

    ospec = pl.BlockSpec((n_sample, HEAD_DIM), lambda h: (0, h))
    oshape = jax.ShapeDtypeStruct((n_sample, d), F32)
    return pl.pallas_call(
        _sample_prep_kernel,
        grid=(n_heads,),
        in_specs=[pspec(0), pspec(1), pspec(2), cspec(0), cspec(1), cspec(2),
                  wspec(0), wspec(1), wspec(2),
                  pl.BlockSpec((n_sample, LANES), lambda h: (rb, 0)),
                  pspec(1), pspec(2),
                  pl.BlockSpec((n_lb, HEAD_DIM), lambda h: (0, h))],
        out_specs=[ospec] * 7,
        out_shape=[oshape] * 7,
        compiler_params=_params(1),
        name="sample_prep",
    )(proj, proj, proj, conv_state, conv_state, conv_state, conv_w, conv_w, conv_w,
      gates, projb, projb, lb_logits)


def _gdn_step_kernel(s_ref, qt_ref, kt_ref, v_ref, eg_ref, beta_ref, pz_ref, nw_ref,
                     so_ref, o_ref, obuf):
    i = pl.program_id(0)
    bt = v_ref.shape[0]
    qt = qt_ref[...]
    kt = kt_ref[...]

    def body(bb, carry):
        bg = i * bt + bb
        kcol = _lane_pick(kt, bg)
        qcol = _lane_pick(qt, bg)
        sd = s_ref[bb] * eg_ref[pl.ds(bb, 1), :]
        ks = jnp.sum(sd * kcol, axis=0, keepdims=True)
        u = beta_ref[pl.ds(bb, 1), :] * (v_ref[pl.ds(bb, 1), :] - ks)
        sn = sd + kcol * u
        so_ref[bb] = sn
        obuf[pl.ds(bb, 1), :] = jnp.sum(sn * qcol, axis=0, keepdims=True)
        return carry

    lax.fori_loop(0, bt, body, 0, unroll=STEP_UNROLL)
    o_ref[...] = _rms_gate(obuf[...], nw_ref[...], pz_ref[...])


def _hgrn_step_kernel(s_ref, qt_ref, ft_ref, pi_ref, pz_ref, nw_ref, so_ref, o_ref, obuf):
    i = pl.program_id(0)
    bt = pi_ref.shape[0]
    qt = qt_ref[...]
    ft = ft_ref[...]

    def body(bb, carry):
        bg = i * bt + bb
        fcol = _lane_pick(ft, bg)
        qcol = _lane_pick(qt, bg)
        sn = fcol * s_ref[bb] + (1.0 - fcol) * pi_ref[pl.ds(bb, 1), :]
        so_ref[bb] = sn
        obuf[pl.ds(bb, 1), :] = jnp.sum(sn * qcol, axis=0, keepdims=True)
        return carry

    lax.fori_loop(0, bt, body, 0, unroll=STEP_UNROLL)
    o_ref[...] = _rms_gate(obuf[...], nw_ref[...], pz_ref[...])


def _sample_step(kind, state, cols_t, rows, projb, norm_w, n_prompt, n_heads):
    n_sample = state.shape[0]
    bt = _pick(n_sample, 16, 8)
    rb = n_prompt // bt
    sspec = pl.BlockSpec((bt, None, HEAD_DIM, HEAD_DIM), lambda i, h: (i, h, 0, 0))
    tspec = pl.BlockSpec((HEAD_DIM, n_sample), lambda i, h: (h, 0))
    rspec = pl.BlockSpec((bt, HEAD_DIM), lambda i, h: (i, h))

    def pspec(seg):
        return pl.BlockSpec((bt, HEAD_DIM), lambda i, h: (rb + i, seg * n_heads + h))

    nspec = pl.BlockSpec((1, HEAD_DIM), lambda i, h: (0, 0))
    if kind == "gdn":
        kern = _gdn_step_kernel
        in_specs = [sspec, tspec, tspec, rspec, rspec, rspec, pspec(0), nspec]
        args = [state, *cols_t, *rows, projb, norm_w]
    else:
        kern = _hgrn_step_kernel
        in_specs = [sspec, tspec, tspec, pspec(3), pspec(4), nspec]
        args = [state, *cols_t, projb, projb, norm_w]
    return pl.pallas_call(
        kern,
        grid=(n_sample // bt, n_heads),
        in_specs=in_specs,
        out_specs=[sspec, rspec],
        out_shape=[jax.ShapeDtypeStruct(state.shape, F32),
                   jax.ShapeDtypeStruct((n_sample, n_heads * HEAD_DIM), F32)],
        scratch_shapes=[pltpu.VMEM((bt, HEAD_DIM), F32)],
        compiler_params=_params(2),
        name=kind + "_step",
    )(*args)


def _out_proj_kernel(ra_ref, rb_ref, oa_ref, oas_ref, ob_ref, obs_ref, x_ref, w_ref, g_ref, b_ref,
                     h_ref, *, alpha, n_prompt_tiles):
    is_prompt = pl.program_id(0) < n_prompt_tiles
    oa = jnp.where(is_prompt, oa_ref[...], oas_ref[...])
    ob = jnp.where(is_prompt, ob_ref[...], obs_ref[...])
    merged = _sigmoid(ra_ref[...]) * oa + _sigmoid(rb_ref[...]) * ob
    mix = _dot(merged.astype(BF16), w_ref[...])
    h_ref[...] = _layer_norm(alpha * x_ref[...] + mix, g_ref[...], b_ref[...])


def _out_projection(projb, oa, oa_s, ob, ob_s, x, w_out, ln_g, ln_b, alpha):
    n, d = x.shape
    n_prompt = oa.shape[0]
    tm = oa_s.shape[0]
    assert n_prompt % tm == 0 and n == n_prompt + tm
    npt = n_prompt // tm
    row = pl.BlockSpec((tm, d), lambda i: (i, 0))
    prow = pl.BlockSpec((tm, d), lambda i: (jnp.minimum(i, npt - 1), 0))
    srow = pl.BlockSpec((tm, d), lambda i: (0, 0))
    vec = pl.BlockSpec((1, d), lambda i: (0, 0))
    return pl.pallas_call(
        functools.partial(_out_proj_kernel, alpha=alpha, n_prompt_tiles=npt),
        grid=(n // tm,),
        in_specs=[pl.BlockSpec((tm, d), lambda i: (i, 5)),
                  pl.BlockSpec((tm, d), lambda i: (i, 6)),
                  prow, srow, prow, srow, row,
                  pl.BlockSpec((d, d), lambda i: (0, 0)), vec, vec],
        out_specs=row,
        out_shape=jax.ShapeDtypeStruct((n, d), F32),
        compiler_params=_params(1),
        name="out_projection",
    )(projb, projb, oa, oa_s, ob, ob_s, x, w_out, ln_g, ln_b)


def _router_kernel(h_ref, w_ref, b_ref, r_ref, cnt_ref, carry, *, n_experts):
    i = pl.program_id(0)

    @pl.when(i == 0)
    def _():
        carry[...] = jnp.zeros_like(carry)

    logits = _dot3(h_ref[...], w_ref[...]) + b_ref[...]
    tm = logits.shape[0]
    lane = _iota(logits.shape, 1)
    x = jnp.where(lane < n_experts, logits, -jnp.inf)
    vals, idxs = [], []
    for _ in range(TOP_K):
        m = jnp.max(x, axis=1, keepdims=True)
        idx = jnp.min(jnp.where(x == m, lane, LANES), axis=1, keepdims=True)
        vals.append(m)
        idxs.append(idx)
        x = jnp.where(lane == idx, -jnp.inf, x)
    es = [jnp.exp(v - vals[0]) for v in vals]
    den = es[0] + es[1] + es[2] + es[3]
    hot = jnp.zeros(logits.shape, F32)
    for idx in idxs:
        hot = hot + jnp.where(lane == idx, 1.0, 0.0)
    r = _iota((tm, tm), 0)
    c = _iota((tm, tm), 1)
    before = _dot(jnp.where(r > c, 1.0, 0.0).astype(BF16), hot.astype(BF16)) + carry[...]
    out = jnp.zeros(logits.shape, F32)
    for k in range(TOP_K):
        rank = jnp.sum(jnp.where(lane == idxs[k], before, 0.0), axis=1, keepdims=True)
        out = out + jnp.where(lane == k, idxs[k].astype(F32), 0.0)
        out = out + jnp.where(lane == TOP_K + k, rank, 0.0)
        out = out + jnp.where(lane == 2 * TOP_K + k, es[k] / den, 0.0)
    r_ref[...] = out
    carry[...] = carry[...] + jnp.sum(hot, axis=0, keepdims=True)
    cnt_ref[...] = carry[...]


def _router(h, w_r, b_r, n_experts):
    n, d = h.shape
    tm = _pick(n, 208, 16)
    return pl.pallas_call(
        functools.partial(_router_kernel, n_experts=n_experts),
        grid=(n // tm,),
        in_specs=[pl.BlockSpec((tm, d), lambda i: (i, 0)),
                  pl.BlockSpec((d, LANES), lambda i: (0, 0)),
                  pl.BlockSpec((1, LANES), lambda i: (0, 0))],
        out_specs=[pl.BlockSpec((tm, LANES), lambda i: (i, 0)),
                   pl.BlockSpec((1, LANES), lambda i: (0, 0))],
        out_shape=[jax.ShapeDtypeStruct((n, LANES), F32),
                   jax.ShapeDtypeStruct((1, LANES), F32)],
        scratch_shapes=[pltpu.VMEM((1, LANES), F32)],
        compiler_params=_params(1),
        name="router",
    )(h, w_r, b_r)


def _dispatch_kernel(dest_ref, h_ref, xin_ref, xb_ref, sem):
    del xin_ref
    i = pl.program_id(0)
    tm = h_ref.shape[0]

    def row_copy(r, k):
        dst = dest_ref[(i * tm + r) * TOP_K + k]
        return pltpu.make_async_copy(h_ref.at[pl.ds(r, 1), :], xb_ref.at[pl.ds(dst, 1), :], sem)

    def start(r, carry):
        for k in range(TOP_K):
            row_copy(r, k).start()
        return carry

    def wait(r, carry):
        for k in range(TOP_K):
            row_copy(r, k).wait()
        return carry

    lax.fori_loop(0, tm, start, 0, unroll=8)
    lax.fori_loop(0, tm, wait, 0, unroll=8)


def _dispatch(dest_flat, h, xb_init):
    n, d = h.shape
    tm = _pick(n, 128, 8)
    return pl.pallas_call(
        _dispatch_kernel,
        grid_spec=pltpu.PrefetchScalarGridSpec(
            num_scalar_prefetch=1,
            grid=(n // tm,),
            in_specs=[pl.BlockSpec((tm, d), lambda i, dest: (i, 0)),
                      pl.BlockSpec(memory_space=pl.ANY)],
            out_specs=pl.BlockSpec(memory_space=pl.ANY),
            scratch_shapes=[pltpu.SemaphoreType.DMA(())]),
        out_shape=jax.ShapeDtypeStruct(xb_init.shape, F32),
        input_output_aliases={2: 0},
        compiler_params=_params(1),
        name="dispatch",
    )(dest_flat, h, xb_init)


def _swiglu_interleaved(h):
    m = h.shape[0]
    lane = _iota((m, LANES), 1)
    low = lane < LANES // 2
    evens_first = jnp.where(low, 2 * lane, 2 * lane - (LANES - 1))
    acts = []
    for p in range(h.shape[1] // (2 * LANES)):
        a = jnp.take_along_axis(h[:, 2 * p * LANES:(2 * p + 1) * LANES], evens_first, axis=1)
        b = jnp.take_along_axis(h[:, (2 * p + 1) * LANES:(2 * p + 2) * LANES], evens_first, axis=1)
        gate = jnp.where(low, a, pltpu.roll(b, LANES // 2, 1))
        up = jnp.where(low, pltpu.roll(a, LANES // 2, 1), b)
        gate = jnp.minimum(gate, SWIGLU_LIMIT)
        up = jnp.clip(up, -SWIGLU_LIMIT, SWIGLU_LIMIT)
        acts.append((up + 1.0) * (gate * _sigmoid(gate * SWIGLU_ALPHA)))
    return jnp.concatenate(acts, axis=1)


def _moe_kernel(tile_ref, exp_ref, blk0_ref, nblk_ref, zero_ref, x_ref, wgu_ref, bgu_ref, wdn_ref,
                bdn_ref, y_ref):
    del tile_ref, exp_ref
    it = pl.program_id(0)
    j = pl.program_id(1)
    blk0 = blk0_ref[it]
    nblk = nblk_ref[it]

    @pl.when(jnp.logical_and(j == 0, zero_ref[it] == 1))
    def _():
        y_ref[...] = jnp.zeros_like(y_ref)

    def run_blocks(n_blocks):
        wg = wgu_ref[...].astype(BF16)
        wd = wdn_ref[...].astype(BF16)
        bgu = bgu_ref[...]
        bdn = bdn_ref[...]
        hs = {}

        def rows_of(i):
            return pl.ds(pl.multiple_of((blk0 + i) * MOE_ROW_BLOCK, MOE_ROW_BLOCK), MOE_ROW_BLOCK)

        def up_proj(i):
            hs[i] = _dot(x_ref[rows_of(i), :].astype(BF16), wg) + bgu

        def down_proj(i):
            contrib = _dot(_swiglu_interleaved(hs.pop(i)).astype(BF16), wd)
            prev = jnp.where(j == 0, jnp.broadcast_to(bdn, contrib.shape), y_ref[rows_of(i), :])
            y_ref[rows_of(i), :] = prev + contrib

        up_proj(0)
        for i in range(1, n_blocks):
            up_proj(i)
            down_proj(i - 1)
        down_proj(n_blocks - 1)

    for n_blocks in range(1, MOE_ROW_TILE // MOE_ROW_BLOCK + 1):
        pl.when(nblk == n_blocks)(functools.partial(run_blocks, n_blocks))


def _moe_gemm(item_tile, item_exp, item_blk0, item_nblk, item_zero, xb, w_gu, b_gu, w_dn, b_dn):
    p_rows = xb.shape[0]
    n_exp, d, two_de = w_gu.shape
    de = two_de // 2
    n_items = item_tile.shape[0]
    nj = two_de // MOE_COL_TILE
    dn_rows = MOE_COL_TILE // 2

    def live_j(it, j, nblk):
        return jnp.where(nblk[it] > 0, j, nj - 1)

    return pl.pallas_call(
        _moe_kernel,
        grid_spec=pltpu.PrefetchScalarGridSpec(
            num_scalar_prefetch=5,
            grid=(n_items, nj),
            in_specs=[
                pl.BlockSpec((MOE_ROW_TILE, d), lambda it, j, tl, ex, b0, nb, zf: (tl[it], 0)),
                pl.BlockSpec((None, d, MOE_COL_TILE),
                             lambda it, j, tl, ex, b0, nb, zf: (ex[it], 0, live_j(it, j, nb))),
                pl.BlockSpec((None, 1, MOE_COL_TILE),
                             lambda it, j, tl, ex, b0, nb, zf: (ex[it], 0, live_j(it, j, nb))),
                pl.BlockSpec((None, dn_rows, d),
                             lambda it, j, tl, ex, b0, nb, zf: (ex[it], live_j(it, j, nb), 0)),
                pl.BlockSpec((None, 1, d), lambda it, j, tl, ex, b0, nb, zf: (ex[it], 0, 0)),
            ],
            out_specs=pl.BlockSpec((MOE_ROW_TILE, d), lambda it, j, tl, ex, b0, nb, zf: (tl[it], 0))),
        out_shape=jax.ShapeDtypeStruct((p_rows, d), F32),
        compiler_params=_params(2),
        name="moe_gemm",
    )(item_tile, item_exp, item_blk0, item_nblk, item_zero, xb, w_gu, b_gu.reshape(n_exp, 1, two_de),
      w_dn, b_dn.reshape(n_exp, 1, d))


def _combine_kernel(dest_ref, h_ref, r_ref, g_ref, b_ref, yb_ref, op_ref, os_ref, rows, sem, *,
                    alpha, n_prompt_tiles):
    i = pl.program_id(0)
    tm, d = h_ref.shape

    def row_copy(r, k):
        src = dest_ref[(i * tm + r) * TOP_K + k]
        return pltpu.make_async_copy(yb_ref.at[pl.ds(src, 1), :], rows.at[k, pl.ds(r, 1), :], sem)

    def start(r, carry):
        for k in range(TOP_K):
            row_copy(r, k).start()
        return carry

    def wait(r, carry):
        for k in range(TOP_K):
            row_copy(r, k).wait()
        return carry

    lax.fori_loop(0, tm, start, 0, unroll=8)
    lax.fori_loop(0, tm, wait, 0, unroll=8)
    rr = r_ref[...]
    ffn = jnp.zeros((tm, d), F32)
    for k in range(TOP_K):
        ffn = ffn + _lane_pick(rr, 2 * TOP_K + k) * rows[k]
    out = _layer_norm(alpha * h_ref[...] + ffn, g_ref[...], b_ref[...])

    @pl.when(i < n_prompt_tiles)
    def _():
        op_ref[...] = out

    @pl.when(i >= n_prompt_tiles)
    def _():
        os_ref[...] = out


def _combine(dest_flat, h, route, ln_g, ln_b, yb, alpha, n_sample):
    n, d = h.shape
    tm = n_sample
    n_prompt = n - n_sample
    assert n_prompt % tm == 0
    npt = n_prompt // tm
    return pl.pallas_call(
        functools.partial(_combine_kernel, alpha=alpha, n_prompt_tiles=npt),
        grid_spec=pltpu.PrefetchScalarGridSpec(
            num_scalar_prefetch=1,
            grid=(n // tm,),
            in_specs=[pl.BlockSpec((tm, d), lambda i, dest: (i, 0)),
                      pl.BlockSpec((tm, LANES), lambda i, dest: (i, 0)),
                      pl.BlockSpec((1, d), lambda i, dest: (0, 0)),
                      pl.BlockSpec((1, d), lambda i, dest: (0, 0)),
                      pl.BlockSpec(memory_space=pl.ANY)],
            out_specs=[pl.BlockSpec((tm, d), lambda i, dest: (jnp.minimum(i, npt - 1), 0)),
                       pl.BlockSpec((tm, d), lambda i, dest: (0, 0))],
            scratch_shapes=[pltpu.VMEM((TOP_K, tm, d), F32),
                            pltpu.SemaphoreType.DMA(())]),
        out_shape=[jax.ShapeDtypeStruct((n_prompt, d), F32),
                   jax.ShapeDtypeStruct((n_sample, d), F32)],
        compiler_params=_params(1),
        name="combine",
    )(dest_flat, h, route, ln_g, ln_b, yb)


def _routing_tables(route, counts_row, n_experts):
    n = route.shape[0]
    e_idx = route[:, 0:TOP_K].astype(I32)
    rank = route[:, TOP_K:2 * TOP_K].astype(I32)
    counts = counts_row[0, :n_experts].astype(I32)
    pcounts = (counts + MOE_ROW_BLOCK - 1) // MOE_ROW_BLOCK * MOE_ROW_BLOCK
    pends = jnp.cumsum(pcounts)
    pstarts = pends - pcounts
    dest = (pstarts[e_idx] + rank).reshape(-1)

    p_rows = -(-(n * TOP_K + n_experts * MOE_ROW_BLOCK) // MOE_ROW_TILE) * MOE_ROW_TILE
    n_blocks = p_rows // MOE_ROW_BLOCK
    bpt = MOE_ROW_TILE // MOE_ROW_BLOCK
    max_items = p_rows // MOE_ROW_TILE + n_experts
    blk = jnp.arange(n_blocks, dtype=I32)
    valid = blk * MOE_ROW_BLOCK < pends[-1]
    blk_e = jnp.minimum(jnp.sum((pends[None, :] <= (blk * MOE_ROW_BLOCK)[:, None]).astype(I32), axis=1),
                        n_experts - 1)
    first = valid & ((blk % bpt == 0) | (blk_e != jnp.roll(blk_e, 1)))
    item_of_blk = jnp.cumsum(first.astype(I32)) - 1
    n_items = jnp.sum(first.astype(I32))
    it = jnp.arange(max_items, dtype=I32)
    mine = item_of_blk[None, :] == it[:, None]
    item_first = jnp.sum(jnp.where(mine & first[None, :], blk[None, :], 0), axis=1)
    item_nblk = jnp.sum((mine & valid[None, :]).astype(I32), axis=1)
    live = it < n_items
    item_first = item_first[jnp.minimum(it, n_items - 1)]
    item_nblk = jnp.where(live, item_nblk, 0)
    item_blk0 = item_first % bpt
    n_tiles = p_rows // MOE_ROW_TILE
    last_tile = item_first[max_items - 1] // bpt
    idle_tile = last_tile + 1 + (it - n_items)
    item_tile = jnp.where(live, item_first // bpt, jnp.minimum(idle_tile, n_tiles - 1))
    item_zero = jnp.where(live, (item_blk0 == 0) & (item_nblk > 0), idle_tile < n_tiles)
    return (dest, p_rows, item_tile, blk_e[item_first], item_blk0, item_nblk,
            item_zero.astype(I32))


def kernel(x_prompt, x_sample, state_gdn_conv, state_gdn_s, state_hgrn_s, w_in, gdn_conv_w,
           gdn_a_log, gdn_dt_bias, gdn_norm_w, hg_lb_logits, hg_norm_w, w_out, ln1_g, ln1_b,
           w_router, b_router, w_gate_up, b_gate_up, w_down, b_down, ln2_g, ln2_b):
    depth = w_in.shape[0]
    assert depth == 1
    batch, seq, d = x_prompt.shape
    n_sample = x_sample.shape[0]
    assert x_sample.shape[1] == 1
    n_heads = d // HEAD_DIM
    n_prompt = batch * seq
    n_rows = n_prompt + n_sample
    n_experts = w_router.shape[-1]
    alpha = (2.0 * depth) ** 0.25
    qkv = 3 * d
    ab0 = qkv
    ab1 = qkv + 2 * n_heads

    x_all = jnp.concatenate([x_prompt.reshape(n_prompt, d), x_sample.reshape(n_sample, d)],
                            axis=0).astype(F32)
    wt = jnp.swapaxes(w_in[0], 0, 1).astype(F32)
    w_ab = jnp.pad(wt[ab0:ab1], ((0, LANES - 2 * n_heads), (0, 0)))
    prm = jnp.zeros((8, LANES), F32)
    prm = prm.at[0, :n_heads].set(gdn_a_log[0].astype(F32))
    prm = prm.at[1, :n_heads].set(gdn_dt_bias[0].astype(F32))
    conv_w = gdn_conv_w[0].astype(F32)
    gdn_nw = gdn_norm_w[0].astype(F32).reshape(1, HEAD_DIM)
    hg_nw = hg_norm_w[0].astype(F32).reshape(1, HEAD_DIM)
    lb_logits = hg_lb_logits.astype(F32)

    x_bf = x_all.astype(BF16)
    proj = _in_projection(x_bf, wt, 0, qkv, 1024)
    projb = _in_projection(x_bf, wt, ab1, wt.shape[0] - ab1, 1024)
    gates, gates_t = _gates(x_all, w_ab, prm, n_prompt, n_heads)
    oa, gdn_s_prompt = _gdn_prompt(proj, projb, gates, gates_t, conv_w, gdn_nw, batch, seq, n_heads)
    ob, hg_s_prompt = _hgrn_prompt(projb, lb_logits, hg_nw, batch, seq, n_heads)

    conv_state = state_gdn_conv[0].astype(F32)
    sq, sk, sv, seg, sbeta, sf, sqh = _sample_prep(proj, projb, conv_state, conv_w, gates,
                                                   lb_logits,
                                                   n_prompt, n_sample, n_heads)
    gdn_s_sample, oa_s = _sample_step("gdn", state_gdn_s[0].astype(F32), (sq.T, sk.T),
                                      (sv, seg, sbeta), projb, gdn_nw, n_prompt, n_heads)
    hg_s_sample, ob_s = _sample_step("hgrn", state_hgrn_s[0].astype(F32), (sqh.T, sf.T), (),
                                     projb, hg_nw, n_prompt, n_heads)

    h = _out_projection(projb, oa, oa_s, ob, ob_s, x_all, w_out[0].astype(BF16),
                        ln1_g[0].astype(F32).reshape(1, d), ln1_b[0].astype(F32).reshape(1, d),
                        alpha)

    w_r = jnp.pad(w_router[0].astype(F32), ((0, 0), (0, LANES - n_experts)))
    b_r = jnp.pad(b_router[0].astype(F32), (0, LANES - n_experts)).reshape(1, LANES)
    route, counts_row = _router(h, w_r, b_r, n_experts)
    dest, p_rows, item_tile, item_exp, item_blk0, item_nblk, item_zero = _routing_tables(
        route, counts_row, n_experts)

    xb = _dispatch(dest, h, jnp.zeros((p_rows, d), F32))
    yb = _moe_gemm(item_tile, item_exp, item_blk0, item_nblk, item_zero, xb, w_gate_up[0],
                   b_gate_up[0], w_down[0], b_down[0])
    yp, ys = _combine(dest, h, route, ln2_g[0].astype(F32).reshape(1, d),
                      ln2_b[0].astype(F32).reshape(1, d), yb, alpha, n_sample)

    y_prompt = yp.reshape(batch, seq, d).astype(x_prompt.dtype)
    y_sample = ys.reshape(n_sample, 1, d).astype(x_sample.dtype)
    new_conv_p = jnp.stack([proj[(b + 1) * seq - 3:(b + 1) * seq] for b in range(batch)])[None]
    new_conv_s = jnp.concatenate([conv_state[:, 1:, :], proj[n_prompt:, None, :]], axis=1)[None]
    sdt = state_gdn_s.dtype
    return (y_prompt, y_sample,
            new_conv_p.astype(state_gdn_conv.dtype), gdn_s_prompt[None].astype(sdt),
            hg_s_prompt[None].astype(state_hgrn_s.dtype),
            new_conv_s.astype(state_gdn_conv.dtype), gdn_s_sample[None].astype(sdt),
            hg_s_sample[None].astype(state_hgrn_s.dtype))
```

```python
import functools

import jax
import jax.numpy as jnp
from jax import lax
from jax.experimental import pallas as pl
from jax.experimental.pallas import tpu as pltpu

F32 = jnp.float32
BF16 = jnp.bfloat16
I32 = jnp.int32
U32 = jnp.uint32

HEAD_DIM = 128
LANES = 128
GDN_CHUNK = 64
HG_SUB = 16
TIME_BLOCK = 256
GDN_HEADS_PER_STEP = 4
HG_HEADS_PER_STEP = 2
STEP_UNROLL = 4
TOP_K = 4
MOE_ROW_BLOCK = 256
MOE_ROW_TILE = 1280
MOE_ALIGN_SLACK_BLOCKS = 16
MOE_VMEM_LIMIT = 60 * 1024 * 1024
MOE_COL_TILE = 512
SWIGLU_LIMIT = 7.0
SWIGLU_ALPHA = 1.702
LN_EPS = 1e-5
RMS_EPS = 1e-6
L2_EPS = 1e-6
VMEM_LIMIT = 56 * 1024 * 1024


def _params(n_axes, vmem=VMEM_LIMIT):
    return pltpu.CompilerParams(dimension_semantics=("arbitrary",) * n_axes,
                                vmem_limit_bytes=vmem)


def _pick(n, target, mult):
    best = None
    for d in range(mult, min(n, target) + 1, mult):
        if n % d == 0:
            best = d
    assert best is not None, (n, target, mult)
    return best


def _dot(a, b):
    return jnp.dot(a, b, preferred_element_type=F32)


def _dot_nt(a, b):
    return lax.dot_general(a, b, (((1,), (1,)), ((), ())), preferred_element_type=F32)


def _dot_tn(a, b):
    return lax.dot_general(a, b, (((0,), (0,)), ((), ())), preferred_element_type=F32)


def _hi_lo(x):
    hi = x.astype(BF16)
    lo = (x - hi.astype(F32)).astype(BF16)
    return hi, lo


def _dot3(a, b):
    ah, al = _hi_lo(a)
    bh, bl = _hi_lo(b)
    return _dot(ah, bh) + (_dot(ah, bl) + _dot(al, bh))


def _dot_exact_lhs(m_bf16, x):
    p1 = x.astype(BF16)
    r1 = x - p1.astype(F32)
    p2 = r1.astype(BF16)
    p3 = (r1 - p2.astype(F32)).astype(BF16)
    return _dot(m_bf16, p1) + (_dot(m_bf16, p2) + _dot(m_bf16, p3))


def _sigmoid(x):
    return 1.0 / (1.0 + jnp.exp(-x))


def _silu(x):
    return x * _sigmoid(x)


def _softplus(x):
    return jnp.maximum(x, 0.0) + jnp.log(1.0 + jnp.exp(-jnp.abs(x)))


def _iota(shape, dim):
    return lax.broadcasted_iota(I32, shape, dim)


def _layer_norm(y, g, b):
    mu = jnp.mean(y, axis=-1, keepdims=True)
    yc = y - mu
    var = jnp.mean(yc * yc, axis=-1, keepdims=True)
    return yc * lax.rsqrt(var + LN_EPS) * g + b


def _rms_gate(o, w, z):
    return o * lax.rsqrt(jnp.mean(o * o, axis=-1, keepdims=True) + RMS_EPS) * w * _silu(z)


def _l2norm(x):
    return x * lax.rsqrt(jnp.sum(x * x, axis=-1, keepdims=True) + L2_EPS)


def _lane_pick(x, idx):
    lane = _iota(x.shape, 1)
    return jnp.sum(jnp.where(lane == idx, x, 0.0), axis=1, keepdims=True)


def _mm_nt_kernel(x_ref, wt_ref, o_ref):
    o_ref[...] = _dot_nt(x_ref[...].astype(BF16), wt_ref[...].astype(BF16))


def _in_projection(x, wt, row0, no, tn_target):
    n, d = x.shape
    tm = _pick(n, 1040, 16)
    tn = _pick(no, tn_target, LANES)
    if row0 % tn == 0:
        wspec = pl.BlockSpec((tn, d), lambda i, j: (row0 // tn + j, 0))
    else:
        assert row0 % 8 == 0
        wspec = pl.BlockSpec((pl.Element(tn), pl.Element(d)),
                             lambda i, j: (pl.multiple_of(row0 + j * tn, 8), 0))
    return pl.pallas_call(
        _mm_nt_kernel,
        grid=(n // tm, no // tn),
        in_specs=[pl.BlockSpec((tm, d), lambda i, j: (i, 0)), wspec],
        out_specs=pl.BlockSpec((tm, tn), lambda i, j: (i, j)),
        out_shape=jax.ShapeDtypeStruct((n, no), F32),
        compiler_params=_params(2),
        name="in_projection",
    )(x, wt)


def _gates_kernel(x_ref, w_ref, prm_ref, g_ref, gt_ref, *, n_prompt_tiles, n_heads):
    i = pl.program_id(0)
    xh, xl = _hi_lo(x_ref[...])
    wh, wl = _hi_lo(w_ref[...])
    ab = _dot_nt(xh, wh) + (_dot_nt(xh, wl) + _dot_nt(xl, wh))
    prm = prm_ref[...]
    g = -jnp.exp(prm[0:1]) * _softplus(ab + prm[1:2])
    beta = _sigmoid(ab)
    tm = ab.shape[0]
    r = _iota((tm, tm), 0)
    c = _iota((tm, tm), 1)
    shift = jnp.where(i < n_prompt_tiles, GDN_CHUNK.bit_length() - 1, 0)
    tri = jnp.logical_and((r >> shift) == (c >> shift), r >= c)
    gc = _dot_exact_lhs(jnp.where(tri, 1.0, 0.0).astype(BF16), g)
    lane = _iota(ab.shape, 1)
    out = jnp.where(lane < n_heads, gc, beta)
    g_ref[...] = out
    gt_ref[...] = out.T


def _gates(x, w_ab, prm, n_prompt, n_heads):
    n, d = x.shape
    tm = LANES
    assert n % tm == 0 and n_prompt % tm == 0 and tm % GDN_CHUNK == 0
    kern = functools.partial(_gates_kernel, n_prompt_tiles=n_prompt // tm, n_heads=n_heads)
    return pl.pallas_call(
        kern,
        grid=(n // tm,),
        in_specs=[pl.BlockSpec((tm, d), lambda i: (i, 0)),
                  pl.BlockSpec((LANES, d), lambda i: (0, 0)),
                  pl.BlockSpec((8, LANES), lambda i: (0, 0))],
        out_specs=[pl.BlockSpec((tm, LANES), lambda i: (i, 0)),
                   pl.BlockSpec((LANES, tm), lambda i: (0, i))],
        out_shape=[jax.ShapeDtypeStruct((n, LANES), F32),
                   jax.ShapeDtypeStruct((LANES, n), F32)],
        compiler_params=_params(1),
        name="gates",
    )(x, w_ab, prm)


def _split_dot3(ah, al, bh, bl):
    return _dot(ah, bh) + (_dot(ah, bl) + _dot(al, bh))


def _unit_lower_inverses(mats):
    c = mats[0].shape[0]
    r = _iota((c, c), 0)
    col = _iota((c, c), 1)
    eye = jnp.where(r == col, 1.0, 0.0)
    pair = jnp.logical_and((r >> 1) == (col >> 1), r > col)
    invs = [eye - jnp.where(pair, a, 0.0) for a in mats]
    level = 2
    while (1 << level) <= c:
        half = level - 1
        mask = jnp.logical_and(
            (r >> level) == (col >> level),
            jnp.logical_and(((r >> half) & 1) == 1, ((col >> half) & 1) == 0))
        lows = [_hi_lo(jnp.where(mask, a, 0.0)) for a in mats]
        inv_s = [_hi_lo(inv) for inv in invs]
        xs = [_split_dot3(lh, ll, ih, il) for (lh, ll), (ih, il) in zip(lows, inv_s)]
        x_s = [_hi_lo(x) for x in xs]
        invs = [inv - _split_dot3(ih, il, xh, xl)
                for inv, (ih, il), (xh, xl) in zip(invs, inv_s, x_s)]
        level += 1
    return invs


def _gdn_prompt_kernel(pq_ref, pk_ref, pv_ref, pz_ref, g_ref, gt_ref, wq_ref, wk_ref, wv_ref,
                       nw_ref, o_ref, s_ref, state, cbuf, qs, ks, vs):
    hg = pl.program_id(1)
    t = pl.program_id(2)
    tb = pq_ref.shape[0]
    hb = state.shape[0]
    n_heads = pl.num_programs(1) * hb
    cl = GDN_CHUNK

    @pl.when(t == 0)
    def _():
        state[...] = jnp.zeros_like(state)
        cbuf[:, 0:8, :] = jnp.zeros((3, 8, hb * HEAD_DIM), F32)

    def conv(idx, u_ref, w_ref):
        u = u_ref[...]
        cbuf[idx, 8:8 + tb, :] = u
        w = w_ref[...]
        y = cbuf[idx, 5:5 + tb, :] * w[0:1, :]
        for j in range(1, 4):
            y = y + cbuf[idx, 5 + j:5 + j + tb, :] * w[j:j + 1, :]
        cbuf[idx, 0:8, :] = u[tb - 8:tb, :]
        return _silu(y)

    qc = conv(0, pq_ref, wq_ref)
    kc = conv(1, pk_ref, wk_ref)
    vs[...] = conv(2, pv_ref, wv_ref)
    for i in range(hb):
        sl = slice(i * HEAD_DIM, (i + 1) * HEAD_DIM)
        qs[:, sl] = _l2norm(qc[:, sl]) * (HEAD_DIM ** -0.5)
        ks[:, sl] = _l2norm(kc[:, sl])

    gall = g_ref[...]
    gt = gt_ref[...]
    gc_cols = [_lane_pick(gall, hg * hb + i) for i in range(hb)]
    beta_cols = [_lane_pick(gall, n_heads + hg * hb + i) for i in range(hb)]

    r = _iota((cl, cl), 0)
    col = _iota((cl, cl), 1)
    incl = r >= col
    strict = r > col

    pairs = [(c, i) for c in range(tb // cl) for i in range(hb)]
    pre = []
    for c, i in pairs:
        rows = slice(c * cl, (c + 1) * cl)
        sl = slice(i * HEAD_DIM, (i + 1) * HEAD_DIM)
        q = qs[rows, sl]
        k = ks[rows, sl]
        v = vs[rows, sl]
        gcc = gc_cols[i][rows, :]
        gcr = gt[i:i + 1, rows]
        bc = beta_cols[i][rows, :]
        decay = jnp.where(incl, jnp.exp(jnp.where(incl, gcc - gcr, 0.0)), 0.0)
        kb = k.astype(BF16)
        a = jnp.where(strict, bc * decay * _dot_nt(kb, kb), 0.0)
        aqk = (_dot_nt(q.astype(BF16), kb) * decay).astype(BF16)
        egc = jnp.exp(gcc)
        g_last = gcc[cl - 1:cl, :]
        pre.append(dict(a=a, aqk=aqk, rhs=_hi_lo(jnp.concatenate([bc * egc * k, bc * v], axis=1)),
                        qd=(q * egc).astype(BF16), ke=(k * jnp.exp(g_last - gcc)).astype(BF16),
                        g_end=jnp.exp(g_last)))
    invs = _unit_lower_inverses([p["a"] for p in pre])
    for p, inv in zip(pre, invs):
        ih, il = _hi_lo(inv)
        wu = _split_dot3(ih, il, *p["rhs"])
        p["w"] = wu[:, :HEAD_DIM].astype(BF16)
        p["u0"] = wu[:, HEAD_DIM:]

    s = [state[i] for i in range(hb)]
    outs = [[] for _ in range(hb)]
    for c in range(tb // cl):
        ps = [pre[c * hb + i] for i in range(hb)]
        sb = [x.astype(BF16) for x in s]
        ws = [_dot(p["w"], b) for p, b in zip(ps, sb)]
        qsd = [_dot(p["qd"], b) for p, b in zip(ps, sb)]
        ub = [(p["u0"] - x).astype(BF16) for p, x in zip(ps, ws)]
        s = [p["g_end"] * x + _dot_tn(p["ke"], u) for p, x, u in zip(ps, s, ub)]
        for i in range(hb):
            outs[i].append(qsd[i] + _dot(ps[i]["aqk"], ub[i]))
    nw = nw_ref[...]
    for i in range(hb):
        sl = slice(i * HEAD_DIM, (i + 1) * HEAD_DIM)
        state[i] = s[i]
        o_ref[:, sl] = _rms_gate(jnp.concatenate(outs[i], axis=0), nw, pz_ref[:, sl])

    @pl.when(t == pl.num_programs(2) - 1)
    def _():
        for i in range(hb):
            s_ref[i] = s[i]


def _gdn_prompt(proj, projb, gates, gates_t, conv_w, norm_w, batch, seq, n_heads):
    tb = _pick(seq, TIME_BLOCK, GDN_CHUNK)
    nt = seq // tb
    hb = _pick(n_heads, GDN_HEADS_PER_STEP, 1)
    ng = n_heads // hb
    d = n_heads * HEAD_DIM
    n_rows = gates_t.shape[1]
    gates_t3 = gates_t[:n_heads].reshape(ng, hb, n_rows)

    def rows(b, h, t):
        return b * nt + t

    def pspec(seg):
        return pl.BlockSpec((tb, hb * HEAD_DIM), lambda b, h, t: (rows(b, h, t), seg * ng + h))

    def wspec(seg):
        return pl.BlockSpec((4, hb * HEAD_DIM), lambda b, h, t: (0, seg * ng + h))

    return pl.pallas_call(
        _gdn_prompt_kernel,
        grid=(batch, ng, nt),
        in_specs=[pspec(0), pspec(1), pspec(2), pspec(0),
                  pl.BlockSpec((tb, LANES), lambda b, h, t: (rows(b, h, t), 0)),
                  pl.BlockSpec((None, hb, tb), lambda b, h, t: (h, 0, rows(b, h, t))),
                  wspec(0), wspec(1), wspec(2),
                  pl.BlockSpec((1, HEAD_DIM), lambda b, h, t: (0, 0))],
        out_specs=[pl.BlockSpec((tb, hb * HEAD_DIM), lambda b, h, t: (rows(b, h, t), h)),
                   pl.BlockSpec((None, hb, HEAD_DIM, HEAD_DIM), lambda b, h, t: (b, h, 0, 0))],
        out_shape=[jax.ShapeDtypeStruct((batch * seq, d), F32),
                   jax.ShapeDtypeStruct((batch, n_heads, HEAD_DIM, HEAD_DIM), F32)],
        scratch_shapes=[pltpu.VMEM((hb, HEAD_DIM, HEAD_DIM), F32),
                        pltpu.VMEM((3, tb + 8, hb * HEAD_DIM), F32),
                        pltpu.VMEM((tb, hb * HEAD_DIM), F32),
                        pltpu.VMEM((tb, hb * HEAD_DIM), F32),
                        pltpu.VMEM((tb, hb * HEAD_DIM), F32)],
        compiler_params=_params(3),
        name="gdn_prompt",
    )(proj, proj, proj, projb, gates, gates_t3, conv_w, conv_w, conv_w, norm_w)


def _lower_bound(logits):
    m = jnp.max(logits, axis=0, keepdims=True)
    e = jnp.exp(logits - m)
    return e[0:1, :] / jnp.sum(e, axis=0, keepdims=True)


def _hgrn_prompt_kernel(pq_ref, pf_ref, pi_ref, pz_ref, lb_ref, nw_ref, o_ref, s_ref,
                        state_t):
    t = pl.program_id(2)
    tb = pq_ref.shape[0]
    hb = state_t.shape[0]

    @pl.when(t == 0)
    def _():
        state_t[...] = jnp.zeros_like(state_t)

    lb = _lower_bound(lb_ref[...])
    f = lb + (1.0 - lb) * _sigmoid(pf_ref[...])
    kk = 1.0 - f
    lf = jnp.log2(f)
    q = _silu(pq_ref[...]) * (HEAD_DIM ** -0.5)
    v = pi_ref[...]

    r = _iota((tb, tb), 0)
    c = _iota((tb, tb), 1)
    sub_shift = HG_SUB.bit_length() - 1
    tri = jnp.logical_and((r >> sub_shift) == (c >> sub_shift), r >= c)
    b = _dot_exact_lhs(jnp.where(tri, 1.0, 0.0).astype(BF16), lf)

    width = hb * HEAD_DIM
    heads = [slice(i * HEAD_DIM, (i + 1) * HEAD_DIM) for i in range(hb)]

    half = HG_SUB // 2
    n_sub = tb // HG_SUB

    def halves(x):
        x4 = x.reshape(n_sub, 2, half, width)
        return x4[:, 0], x4[:, 1]

    q_lo, q_hi = halves(q)
    b_lo, b_hi = halves(b)
    k_lo, k_hi = halves(kk)
    v_lo, v_hi = halves(v)
    row = _iota((1, half, 1), 1)

    def rot(x, d):
        return x if d == 0 else pltpu.roll(x, d, 1)

    def add_terms(acc, qx, bx, kp, bp, vp, ok):
        prod = qx * kp * jnp.exp2(bx - bp)
        for i, sl in enumerate(heads):
            w = jnp.sum(prod[:, :, sl], axis=2, keepdims=True)
            if ok is not None:
                w = jnp.where(ok, w, 0.0)
            acc[i] = acc[i] + w * vp[:, :, sl]

    acc_lo = [jnp.zeros((n_sub, half, HEAD_DIM), F32) for _ in range(hb)]
    acc_hi = [jnp.zeros((n_sub, half, HEAD_DIM), F32) for _ in range(hb)]
    for d in range(half):
        ok = None if d == 0 else row >= d
        kl, bl_, vl = rot(k_lo, d), rot(b_lo, d), rot(v_lo, d)
        add_terms(acc_lo, q_lo, b_lo, kl, bl_, vl, ok)
        add_terms(acc_hi, q_hi, b_hi, kl, bl_, vl, ok)
        if d == 0:
            add_terms(acc_hi, q_hi, b_hi, k_hi, b_hi, v_hi, None)
        else:
            add_terms(acc_hi, q_hi, b_hi, jnp.where(ok, rot(k_hi, d), kl),
                      jnp.where(ok, rot(b_hi, d), bl_), jnp.where(ok, rot(v_hi, d), vl), None)
    o_intra = [jnp.stack([lo_, hi_], axis=1).reshape(tb, HEAD_DIM)
               for lo_, hi_ in zip(acc_lo, acc_hi)]

    n_sub = tb // HG_SUB
    eb = jnp.exp2(b)
    qe = (q * eb).astype(BF16)
    incs, scales = [], []
    for j in range(n_sub):
        rows = slice(j * HG_SUB, (j + 1) * HG_SUB)
        bl = b[(j + 1) * HG_SUB - 1:(j + 1) * HG_SUB, :]
        ke = (kk[rows, :] * jnp.exp2(bl - b[rows, :])).astype(BF16)
        vb = v[rows, :].astype(BF16)
        incs.append([_dot_tn(vb[:, sl], ke[:, sl]) for sl in heads])
        scales.append(eb[(j + 1) * HG_SUB - 1:(j + 1) * HG_SUB, :])
    st = [state_t[i] for i in range(hb)]
    before = []
    for j in range(n_sub):
        before.append([x.astype(BF16) for x in st])
        st = [x * scales[j][:, sl] + inc for x, sl, inc in zip(st, heads, incs[j])]
    nw = nw_ref[...]
    for i, sl in enumerate(heads):
        outs = [_dot_nt(qe[j * HG_SUB:(j + 1) * HG_SUB, sl], before[j][i]) for j in range(n_sub)]
        state_t[i] = st[i]
        o_ref[:, sl] = _rms_gate(o_intra[i] + jnp.concatenate(outs, axis=0), nw, pz_ref[:, sl])

    @pl.when(t == pl.num_programs(2) - 1)
    def _():
        for i in range(hb):
            s_ref[i] = st[i].T


def _hgrn_prompt(projb, lb_logits, norm_w, batch, seq, n_heads):
    tb = _pick(seq, TIME_BLOCK, HG_SUB)
    nt = seq // tb
    hb = _pick(n_heads, HG_HEADS_PER_STEP, 1)
    ng = n_heads // hb
    d = n_heads * HEAD_DIM
    n_lb = lb_logits.shape[0]
    width = hb * HEAD_DIM

    def pspec(seg):
        return pl.BlockSpec((tb, width), lambda b, h, t: (b * nt + t, seg * ng + h))

    return pl.pallas_call(
        _hgrn_prompt_kernel,
        grid=(batch, ng, nt),
        in_specs=[pspec(1), pspec(2), pspec(3), pspec(4),
                  pl.BlockSpec((n_lb, width), lambda b, h, t: (0, h)),
                  pl.BlockSpec((1, HEAD_DIM), lambda b, h, t: (0, 0))],
        out_specs=[pl.BlockSpec((tb, width), lambda b, h, t: (b * nt + t, h)),
                   pl.BlockSpec((None, hb, HEAD_DIM, HEAD_DIM), lambda b, h, t: (b, h, 0, 0))],
        out_shape=[jax.ShapeDtypeStruct((batch * seq, d), F32),
                   jax.ShapeDtypeStruct((batch, n_heads, HEAD_DIM, HEAD_DIM), F32)],
        scratch_shapes=[pltpu.VMEM((hb, HEAD_DIM, HEAD_DIM), F32)],
        compiler_params=_params(3),
        name="hgrn_prompt",
    )(projb, projb, projb, projb, lb_logits, norm_w)


def _sample_prep_kernel(pq_ref, pk_ref, pv_ref, cq_ref, ck_ref, cv_ref, wq_ref, wk_ref, wv_ref,
                        g_ref, hq_ref, hf_ref, lb_ref,
                        q_ref, k_ref, v_ref, eg_ref, beta_ref, f_ref, qh_ref):
    h = pl.program_id(0)

    def conv(u_ref, c_ref, w_ref):
        w = w_ref[...]
        y = u_ref[...] * w[3:4, :]
        for j in range(3):
            y = y + c_ref[:, j, :] * w[j:j + 1, :]
        return _silu(y)

    q_ref[...] = _l2norm(conv(pq_ref, cq_ref, wq_ref)) * (HEAD_DIM ** -0.5)
    k_ref[...] = _l2norm(conv(pk_ref, ck_ref, wk_ref))
    v_ref[...] = conv(pv_ref, cv_ref, wv_ref)
    gall = g_ref[...]
    shape = q_ref.shape
    eg_ref[...] = jnp.broadcast_to(jnp.exp(_lane_pick(gall, h)), shape)
    beta_ref[...] = jnp.broadcast_to(_lane_pick(gall, h + pl.num_programs(0)), shape)
    lb = _lower_bound(lb_ref[...])
    f_ref[...] = lb + (1.0 - lb) * _sigmoid(hf_ref[...])
    qh_ref[...] = _silu(hq_ref[...]) * (HEAD_DIM ** -0.5)


def _sample_prep(proj, projb, conv_state, conv_w, gates, lb_logits, n_prompt, n_sample, n_heads):
    assert n_prompt % n_sample == 0
    rb = n_prompt // n_sample
    d = n_heads * HEAD_DIM
    n_lb = lb_logits.shape[0]

    def pspec(seg):
        return pl.BlockSpec((n_sample, HEAD_DIM), lambda h: (rb, seg * n_heads + h))

    def cspec(seg):
        return pl.BlockSpec((n_sample, 3, HEAD_DIM), lambda h: (0, 0, seg * n_heads + h))

    def wspec(seg):
        return pl.BlockSpec((4, HEAD_DIM), lambda h: (0, seg * n_heads + h))

    ospec = pl.BlockSpec((n_sample, HEAD_DIM), lambda h: (0, h))
    oshape = jax.ShapeDtypeStruct((n_sample, d), F32)
    return pl.pallas_call(
        _sample_prep_kernel,
        grid=(n_heads,),
        in_specs=[pspec(0), pspec(1), pspec(2), cspec(0), cspec(1), cspec(2),
                  wspec(0), wspec(1), wspec(2),
                  pl.BlockSpec((n_sample, LANES), lambda h: (rb, 0)),
                  pspec(1), pspec(2),
                  pl.BlockSpec((n_lb, HEAD_DIM), lambda h: (0, h))],
        out_specs=[ospec] * 7,
        out_shape=[oshape] * 7,
        compiler_params=_params(1),
        name="sample_prep",
    )(proj, proj, proj, conv_state, conv_state, conv_state, conv_w, conv_w, conv_w,
      gates, projb, projb, lb_logits)


def _gdn_step_kernel(s_ref, qt_ref, kt_ref, v_ref, eg_ref, beta_ref, pz_ref, nw_ref,
                     so_ref, o_ref, obuf):
    i = pl.program_id(0)
    bt = v_ref.shape[0]
    qt = qt_ref[...]
    kt = kt_ref[...]

    def body(bb, carry):
        bg = i * bt + bb
        kcol = _lane_pick(kt, bg)
        qcol = _lane_pick(qt, bg)
        sd = s_ref[bb] * eg_ref[pl.ds(bb, 1), :]
        ks = jnp.sum(sd * kcol, axis=0, keepdims=True)
        u = beta_ref[pl.ds(bb, 1), :] * (v_ref[pl.ds(bb, 1), :] - ks)
        sn = sd + kcol * u
        so_ref[bb] = sn
        obuf[pl.ds(bb, 1), :] = jnp.sum(sn * qcol, axis=0, keepdims=True)
        return carry

    lax.fori_loop(0, bt, body, 0, unroll=STEP_UNROLL)
    o_ref[...] = _rms_gate(obuf[...], nw_ref[...], pz_ref[...])


def _hgrn_step_kernel(s_ref, qt_ref, ft_ref, pi_ref, pz_ref, nw_ref, so_ref, o_ref, obuf):
    i = pl.program_id(0)
    bt = pi_ref.shape[0]
    qt = qt_ref[...]
    ft = ft_ref[...]

    def body(bb, carry):
        bg = i * bt + bb
        fcol = _lane_pick(ft, bg)
        qcol = _lane_pick(qt, bg)
        sn = fcol * s_ref[bb] + (1.0 - fcol) * pi_ref[pl.ds(bb, 1), :]
        so_ref[bb] = sn
        obuf[pl.ds(bb, 1), :] = jnp.sum(sn * qcol, axis=0, keepdims=True)
        return carry

    lax.fori_loop(0, bt, body, 0, unroll=STEP_UNROLL)
    o_ref[...] = _rms_gate(obuf[...], nw_ref[...], pz_ref[...])


def _sample_step(kind, state, cols_t, rows, projb, norm_w, n_prompt, n_heads):
    n_sample = state.shape[0]
    bt = _pick(n_sample, 16, 8)
    rb = n_prompt // bt
    sspec = pl.BlockSpec((bt, None, HEAD_DIM, HEAD_DIM), lambda i, h: (i, h, 0, 0))
    tspec = pl.BlockSpec((HEAD_DIM, n_sample), lambda i, h: (h, 0))
    rspec = pl.BlockSpec((bt, HEAD_DIM), lambda i, h: (i, h))

    def pspec(seg):
        return pl.BlockSpec((bt, HEAD_DIM), lambda i, h: (rb + i, seg * n_heads + h))

    nspec = pl.BlockSpec((1, HEAD_DIM), lambda i, h: (0, 0))
    if kind == "gdn":
        kern = _gdn_step_kernel
        in_specs = [sspec, tspec, tspec, rspec, rspec, rspec, pspec(0), nspec]
        args = [state, *cols_t, *rows, projb, norm_w]
    else:
        kern = _hgrn_step_kernel
        in_specs = [sspec, tspec, tspec, pspec(3), pspec(4), nspec]
        args = [state, *cols_t, projb, projb, norm_w]
    return pl.pallas_call(
        kern,
        grid=(n_sample // bt, n_heads),
        in_specs=in_specs,
        out_specs=[sspec, rspec],
        out_shape=[jax.ShapeDtypeStruct(state.shape, F32),
                   jax.ShapeDtypeStruct((n_sample, n_heads * HEAD_DIM), F32)],
        scratch_shapes=[pltpu.VMEM((bt, HEAD_DIM), F32)],
        compiler_params=_params(2),
        name=kind + "_step",
    )(*args)


def _out_proj_kernel(ra_ref, rb_ref, oa_ref, oas_ref, ob_ref, obs_ref, x_ref, w_ref, g_ref, b_ref,
                     h_ref, *, alpha, n_prompt_tiles):
    is_prompt = pl.program_id(0) < n_prompt_tiles
    oa = jnp.where(is_prompt, oa_ref[...], oas_ref[...])
    ob = jnp.where(is_prompt, ob_ref[...], obs_ref[...])
    merged = _sigmoid(ra_ref[...]) * oa + _sigmoid(rb_ref[...]) * ob
    mix = _dot(merged.astype(BF16), w_ref[...])
    h_ref[...] = _layer_norm(alpha * x_ref[...] + mix, g_ref[...], b_ref[...])


def _out_projection(projb, oa, oa_s, ob, ob_s, x, w_out, ln_g, ln_b, alpha):
    n, d = x.shape
    n_prompt = oa.shape[0]
    tm = oa_s.shape[0]
    assert n_prompt % tm == 0 and n == n_prompt + tm
    npt = n_prompt // tm
    row = pl.BlockSpec((tm, d), lambda i: (i, 0))
    prow = pl.BlockSpec((tm, d), lambda i: (jnp.minimum(i, npt - 1), 0))
    srow = pl.BlockSpec((tm, d), lambda i: (0, 0))
    vec = pl.BlockSpec((1, d), lambda i: (0, 0))
    return pl.pallas_call(
        functools.partial(_out_proj_kernel, alpha=alpha, n_prompt_tiles=npt),
        grid=(n // tm,),
        in_specs=[pl.BlockSpec((tm, d), lambda i: (i, 5)),
                  pl.BlockSpec((tm, d), lambda i: (i, 6)),
                  prow, srow, prow, srow, row,
                  pl.BlockSpec((d, d), lambda i: (0, 0)), vec, vec],
        out_specs=row,
        out_shape=jax.ShapeDtypeStruct((n, d), F32),
        compiler_params=_params(1),
        name="out_projection",
    )(projb, projb, oa, oa_s, ob, ob_s, x, w_out, ln_g, ln_b)


def _router_kernel(h_ref, w_ref, b_ref, r_ref, cnt_ref, carry, *, n_experts):
    i = pl.program_id(0)

    @pl.when(i == 0)
    def _():
        carry[...] = jnp.zeros_like(carry)

    logits = _dot3(h_ref[...], w_ref[...]) + b_ref[...]
    tm = logits.shape[0]
    lane = _iota(logits.shape, 1)
    x = jnp.where(lane < n_experts, logits, -jnp.inf)
    vals, idxs = [], []
    for _ in range(TOP_K):
        m = jnp.max(x, axis=1, keepdims=True)
        idx = jnp.min(jnp.where(x == m, lane, LANES), axis=1, keepdims=True)
        vals.append(m)
        idxs.append(idx)
        x = jnp.where(lane == idx, -jnp.inf, x)
    es = [jnp.exp(v - vals[0]) for v in vals]
    den = es[0] + es[1] + es[2] + es[3]
    hot = jnp.zeros(logits.shape, F32)
    for idx in idxs:
        hot = hot + jnp.where(lane == idx, 1.0, 0.0)
    r = _iota((tm, tm), 0)
    c = _iota((tm, tm), 1)
    before = _dot(jnp.where(r > c, 1.0, 0.0).astype(BF16), hot.astype(BF16)) + carry[...]
    out = jnp.zeros(logits.shape, F32)
    for k in range(TOP_K):
        rank = jnp.sum(jnp.where(lane == idxs[k], before, 0.0), axis=1, keepdims=True)
        out = out + jnp.where(lane == k, idxs[k].astype(F32), 0.0)
        out = out + jnp.where(lane == TOP_K + k, rank, 0.0)
        out = out + jnp.where(lane == 2 * TOP_K + k, es[k] / den, 0.0)
    r_ref[...] = out
    carry[...] = carry[...] + jnp.sum(hot, axis=0, keepdims=True)
    cnt_ref[...] = carry[...]


def _router(h, w_r, b_r, n_experts):
    n, d = h.shape
    tm = _pick(n, 208, 16)
    return pl.pallas_call(
        functools.partial(_router_kernel, n_experts=n_experts),
        grid=(n // tm,),
        in_specs=[pl.BlockSpec((tm, d), lambda i: (i, 0)),
                  pl.BlockSpec((d, LANES), lambda i: (0, 0)),
                  pl.BlockSpec((1, LANES), lambda i: (0, 0))],
        out_specs=[pl.BlockSpec((tm, LANES), lambda i: (i, 0)),
                   pl.BlockSpec((1, LANES), lambda i: (0, 0))],
        out_shape=[jax.ShapeDtypeStruct((n, LANES), F32),
                   jax.ShapeDtypeStruct((1, LANES), F32)],
        scratch_shapes=[pltpu.VMEM((1, LANES), F32)],
        compiler_params=_params(1),
        name="router",
    )(h, w_r, b_r)


def _dispatch_kernel(dest_ref, h_ref, xin_ref, xb_ref, sem):
    del xin_ref
    i = pl.program_id(0)
    tm = h_ref.shape[0]

    def row_copy(r, k):
        dst = dest_ref[(i * tm + r) * TOP_K + k]
        return pltpu.make_async_copy(h_ref.at[pl.ds(r, 1), :], xb_ref.at[pl.ds(dst, 1), :], sem)

    def start(r, carry):
        for k in range(TOP_K):
            row_copy(r, k).start()
        return carry

    def wait(r, carry):
        for k in range(TOP_K):
            row_copy(r, k).wait()
        return carry

    lax.fori_loop(0, tm, start, 0, unroll=8)
    lax.fori_loop(0, tm, wait, 0, unroll=8)


def _dispatch(dest_flat, h, xb_init):
    n, d = h.shape
    tm = _pick(n, 128, 8)
    return pl.pallas_call(
        _dispatch_kernel,
        grid_spec=pltpu.PrefetchScalarGridSpec(
            num_scalar_prefetch=1,
            grid=(n // tm,),
            in_specs=[pl.BlockSpec((tm, d), lambda i, dest: (i, 0)),
                      pl.BlockSpec(memory_space=pl.ANY)],
            out_specs=pl.BlockSpec(memory_space=pl.ANY),
            scratch_shapes=[pltpu.SemaphoreType.DMA(())]),
        out_shape=jax.ShapeDtypeStruct(xb_init.shape, F32),
        input_output_aliases={2: 0},
        compiler_params=_params(1),
        name="dispatch",
    )(dest_flat, h, xb_init)


def _swiglu_interleaved(h):
    m = h.shape[0]
    lane = _iota((m, LANES), 1)
    low = lane < LANES // 2
    evens_first = jnp.where(low, 2 * lane, 2 * lane - (LANES - 1))
    acts = []
    for p in range(h.shape[1] // (2 * LANES)):
        a = jnp.take_along_axis(h[:, 2 * p * LANES:(2 * p + 1) * LANES], evens_first, axis=1)
        b = jnp.take_along_axis(h[:, (2 * p + 1) * LANES:(2 * p + 2) * LANES], evens_first, axis=1)
        gate = jnp.where(low, a, pltpu.roll(b, LANES // 2, 1))
        up = jnp.where(low, pltpu.roll(a, LANES // 2, 1), b)
        gate = jnp.minimum(gate, SWIGLU_LIMIT)
        up = jnp.clip(up, -SWIGLU_LIMIT, SWIGLU_LIMIT)
        acts.append((up + 1.0) * (gate * _sigmoid(gate * SWIGLU_ALPHA)))
    return jnp.concatenate(acts, axis=1)


def _moe_kernel(tile_ref, exp_ref, blk0_ref, nblk_ref, zero_ref, x_ref, wgu_ref, bgu_ref, wdn_ref,
                bdn_ref, y_ref):
    del tile_ref, exp_ref
    it = pl.program_id(0)
    j = pl.program_id(1)
    blk0 = blk0_ref[it]
    nblk = nblk_ref[it]

    @pl.when(jnp.logical_and(j == 0, zero_ref[it] == 1))
    def _():
        y_ref[...] = jnp.zeros_like(y_ref)

    def run_blocks(n_blocks):
        wg = wgu_ref[...].astype(BF16)
        wd = wdn_ref[...].astype(BF16)
        bgu = bgu_ref[...]
        bdn = bdn_ref[...]
        hs = {}

        def rows_of(i):
            return pl.ds(pl.multiple_of((blk0 + i) * MOE_ROW_BLOCK, MOE_ROW_BLOCK), MOE_ROW_BLOCK)

        def up_proj(i):
            hs[i] = _dot(x_ref[rows_of(i), :].astype(BF16), wg) + bgu

        def down_proj(i):
            contrib = _dot(_swiglu_interleaved(hs.pop(i)).astype(BF16), wd)
            prev = jnp.where(j == 0, jnp.broadcast_to(bdn, contrib.shape), y_ref[rows_of(i), :])
            y_ref[rows_of(i), :] = prev + contrib

        up_proj(0)
        for i in range(1, n_blocks):
            up_proj(i)
            down_proj(i - 1)
        down_proj(n_blocks - 1)

    for n_blocks in range(1, MOE_ROW_TILE // MOE_ROW_BLOCK + 1):
        pl.when(nblk == n_blocks)(functools.partial(run_blocks, n_blocks))


def _moe_gemm(item_tile, item_exp, item_blk0, item_nblk, item_zero, xb, w_gu, b_gu, w_dn, b_dn):
    p_rows = xb.shape[0]
    n_exp, d, two_de = w_gu.shape
    de = two_de // 2
    n_items = item_tile.shape[0]
    nj = two_de // MOE_COL_TILE
    dn_rows = MOE_COL_TILE // 2

    def live_j(it, j, nblk):
        return jnp.where(nblk[it] > 0, j, nj - 1)

    return pl.pallas_call(
        _moe_kernel,
        grid_spec=pltpu.PrefetchScalarGridSpec(
            num_scalar_prefetch=5,
            grid=(n_items, nj),
            in_specs=[
                pl.BlockSpec((MOE_ROW_TILE, d), lambda it, j, tl, ex, b0, nb, zf: (tl[it], 0)),
                pl.BlockSpec((None, d, MOE_COL_TILE),
                             lambda it, j, tl, ex, b0, nb, zf: (ex[it], 0, live_j(it, j, nb))),
                pl.BlockSpec((None, 1, MOE_COL_TILE),
                             lambda it, j, tl, ex, b0, nb, zf: (ex[it], 0, live_j(it, j, nb))),
                pl.BlockSpec((None, dn_rows, d),
                             lambda it, j, tl, ex, b0, nb, zf: (ex[it], live_j(it, j, nb), 0)),
                pl.BlockSpec((None, 1, d), lambda it, j, tl, ex, b0, nb, zf: (ex[it], 0, 0)),
            ],
            out_specs=pl.BlockSpec((MOE_ROW_TILE, d), lambda it, j, tl, ex, b0, nb, zf: (tl[it], 0))),
        out_shape=jax.ShapeDtypeStruct((p_rows, d), F32),
        compiler_params=_params(2, MOE_VMEM_LIMIT),
        name="moe_gemm",
    )(item_tile, item_exp, item_blk0, item_nblk, item_zero, xb, w_gu, b_gu.reshape(n_exp, 1, two_de),
      w_dn, b_dn.reshape(n_exp, 1, d))


def _combine_kernel(dest_ref, h_ref, r_ref, g_ref, b_ref, yb_ref, op_ref, os_ref, rows, sem, *,
                    alpha, n_prompt_tiles):
    i = pl.program_id(0)
    tm, d = h_ref.shape

    def row_copy(r, k):
        src = dest_ref[(i * tm + r) * TOP_K + k]
        return pltpu.make_async_copy(yb_ref.at[pl.ds(src, 1), :], rows.at[k, pl.ds(r, 1), :], sem)

    def start(r, carry):
        for k in range(TOP_K):
            row_copy(r, k).start()
        return carry

    def wait(r, carry):
        for k in range(TOP_K):
            row_copy(r, k).wait()
        return carry

    lax.fori_loop(0, tm, start, 0, unroll=8)
    lax.fori_loop(0, tm, wait, 0, unroll=8)
    rr = r_ref[...]
    ffn = jnp.zeros((tm, d), F32)
    for k in range(TOP_K):
        ffn = ffn + _lane_pick(rr, 2 * TOP_K + k) * rows[k]
    out = _layer_norm(alpha * h_ref[...] + ffn, g_ref[...], b_ref[...])

    @pl.when(i < n_prompt_tiles)
    def _():
        op_ref[...] = out

    @pl.when(i >= n_prompt_tiles)
    def _():
        os_ref[...] = out


def _combine(dest_flat, h, route, ln_g, ln_b, yb, alpha, n_sample):
    n, d = h.shape
    tm = n_sample
    n_prompt = n - n_sample
    assert n_prompt % tm == 0
    npt = n_prompt // tm
    return pl.pallas_call(
        functools.partial(_combine_kernel, alpha=alpha, n_prompt_tiles=npt),
        grid_spec=pltpu.PrefetchScalarGridSpec(
            num_scalar_prefetch=1,
            grid=(n // tm,),
            in_specs=[pl.BlockSpec((tm, d), lambda i, dest: (i, 0)),
                      pl.BlockSpec((tm, LANES), lambda i, dest: (i, 0)),
                      pl.BlockSpec((1, d), lambda i, dest: (0, 0)),
                      pl.BlockSpec((1, d), lambda i, dest: (0, 0)),
                      pl.BlockSpec(memory_space=pl.ANY)],
            out_specs=[pl.BlockSpec((tm, d), lambda i, dest: (jnp.minimum(i, npt - 1), 0)),
                       pl.BlockSpec((tm, d), lambda i, dest: (0, 0))],
            scratch_shapes=[pltpu.VMEM((TOP_K, tm, d), F32),
                            pltpu.SemaphoreType.DMA(())]),
        out_shape=[jax.ShapeDtypeStruct((n_prompt, d), F32),
                   jax.ShapeDtypeStruct((n_sample, d), F32)],
        compiler_params=_params(1),
        name="combine",
    )(dest_flat, h, route, ln_g, ln_b, yb)


def _routing_tables(route, counts_row, n_experts):
    n = route.shape[0]
    e_idx = route[:, 0:TOP_K].astype(I32)
    rank = route[:, TOP_K:2 * TOP_K].astype(I32)
    counts = counts_row[0, :n_experts].astype(I32)
    bpt = MOE_ROW_TILE // MOE_ROW_BLOCK
    blocks_e = (counts + MOE_ROW_BLOCK - 1) // MOE_ROW_BLOCK
    starts = []
    pos = jnp.int32(0)
    skipped = jnp.int32(0)
    for e in range(n_experts):
        off = pos % bpt
        gap = jnp.where((off > 0) & (off + blocks_e[e] > bpt), bpt - off, 0)
        gap = jnp.where(skipped + gap <= MOE_ALIGN_SLACK_BLOCKS, gap, 0)
        starts.append(pos + gap)
        pos = pos + gap + blocks_e[e]
        skipped = skipped + gap
    start_blk = jnp.stack(starts)
    end_blk = start_blk + blocks_e
    dest = (start_blk[e_idx] * MOE_ROW_BLOCK + rank).reshape(-1)

    max_blocks = -(-(n * TOP_K + n_experts * (MOE_ROW_BLOCK - 1)) // MOE_ROW_BLOCK)
    p_rows = -(-(max_blocks + MOE_ALIGN_SLACK_BLOCKS) // bpt) * MOE_ROW_TILE
    n_blocks = p_rows // MOE_ROW_BLOCK
    max_items = p_rows // MOE_ROW_TILE + n_experts
    blk = jnp.arange(n_blocks, dtype=I32)
    inside = (blk[:, None] >= start_blk[None, :]) & (blk[:, None] < end_blk[None, :])
    valid = jnp.any(inside, axis=1)
    blk_e = jnp.sum(jnp.where(inside, jnp.arange(n_experts, dtype=I32)[None, :], 0), axis=1)
    first = valid & ((blk % bpt == 0) | (blk_e != jnp.roll(blk_e, 1)) | ~jnp.roll(valid, 1))
    item_of_blk = jnp.cumsum(first.astype(I32)) - 1
    n_items = jnp.sum(first.astype(I32))
    it = jnp.arange(max_items, dtype=I32)
    mine = item_of_blk[None, :] == it[:, None]
    item_first = jnp.sum(jnp.where(mine & first[None, :], blk[None, :], 0), axis=1)
    item_nblk = jnp.sum((mine & valid[None, :]).astype(I32), axis=1)
    live = it < n_items
    item_first = item_first[jnp.minimum(it, n_items - 1)]
    item_nblk = jnp.where(live, item_nblk, 0)
    item_blk0 = item_first % bpt
    n_tiles = p_rows // MOE_ROW_TILE
    last_tile = item_first[max_items - 1] // bpt
    idle_tile = last_tile + 1 + (it - n_items)
    item_tile = jnp.where(live, item_first // bpt, jnp.minimum(idle_tile, n_tiles - 1))
    item_zero = jnp.where(live, (item_blk0 == 0) & (item_nblk > 0), idle_tile < n_tiles)
    return (dest, p_rows, item_tile, blk_e[item_first], item_blk0, item_nblk,
            item_zero.astype(I32))


def kernel(x_prompt, x_sample, state_gdn_conv, state_gdn_s, state_hgrn_s, w_in, gdn_conv_w,
           gdn_a_log, gdn_dt_bias, gdn_norm_w, hg_lb_logits, hg_norm_w, w_out, ln1_g, ln1_b,
           w_router, b_router, w_gate_up, b_gate_up, w_down, b_down, ln2_g, ln2_b):
    depth = w_in.shape[0]
    assert depth == 1
    batch, seq, d = x_prompt.shape
    n_sample = x_sample.shape[0]
    assert x_sample.shape[1] == 1
    n_heads = d // HEAD_DIM
    n_prompt = batch * seq
    n_rows = n_prompt + n_sample
    n_experts = w_router.shape[-1]
    alpha = (2.0 * depth) ** 0.25
    qkv = 3 * d
    ab0 = qkv
    ab1 = qkv + 2 * n_heads

    x_all = jnp.concatenate([x_prompt.reshape(n_prompt, d), x_sample.reshape(n_sample, d)],
                            axis=0).astype(F32)
    wt = jnp.swapaxes(w_in[0], 0, 1).astype(F32)
    w_ab = jnp.pad(wt[ab0:ab1], ((0, LANES - 2 * n_heads), (0, 0)))
    prm = jnp.zeros((8, LANES), F32)
    prm = prm.at[0, :n_heads].set(gdn_a_log[0].astype(F32))
    prm = prm.at[1, :n_heads].set(gdn_dt_bias[0].astype(F32))
    conv_w = gdn_conv_w[0].astype(F32)
    gdn_nw = gdn_norm_w[0].astype(F32).reshape(1, HEAD_DIM)
    hg_nw = hg_norm_w[0].astype(F32).reshape(1, HEAD_DIM)
    lb_logits = hg_lb_logits.astype(F32)

    x_bf = x_all.astype(BF16)
    proj = _in_projection(x_bf, wt, 0, qkv, 1024)
    projb = _in_projection(x_bf, wt, ab1, wt.shape[0] - ab1, 1024)
    gates, gates_t = _gates(x_all, w_ab, prm, n_prompt, n_heads)
    oa, gdn_s_prompt = _gdn_prompt(proj, projb, gates, gates_t, conv_w, gdn_nw, batch, seq, n_heads)
    ob, hg_s_prompt = _hgrn_prompt(projb, lb_logits, hg_nw, batch, seq, n_heads)

    conv_state = state_gdn_conv[0].astype(F32)
    sq, sk, sv, seg, sbeta, sf, sqh = _sample_prep(proj, projb, conv_state, conv_w, gates,
                                                   lb_logits,
                                                   n_prompt, n_sample, n_heads)
    gdn_s_sample, oa_s = _sample_step("gdn", state_gdn_s[0].astype(F32), (sq.T, sk.T),
                                      (sv, seg, sbeta), projb, gdn_nw, n_prompt, n_heads)
    hg_s_sample, ob_s = _sample_step("hgrn", state_hgrn_s[0].astype(F32), (sqh.T, sf.T), (),
                                     projb, hg_nw, n_prompt, n_heads)

    h = _out_projection(projb, oa, oa_s, ob, ob_s, x_all, w_out[0].astype(BF16),
                        ln1_g[0].astype(F32).reshape(1, d), ln1_b[0].astype(F32).reshape(1, d),
                        alpha)

    w_r = jnp.pad(w_router[0].astype(F32), ((0, 0), (0, LANES - n_experts)))
    b_r = jnp.pad(b_router[0].astype(F32), (0, LANES - n_experts)).reshape(1, LANES)
    route, counts_row = _router(h, w_r, b_r, n_experts)
    dest, p_rows, item_tile, item_exp, item_blk0, item_nblk, item_zero = _routing_tables(
        route, counts_row, n_experts)

    xb = _dispatch(dest, h, jnp.zeros((p_rows, d), F32))
    yb = _moe_gemm(item_tile, item_exp, item_blk0, item_nblk, item_zero, xb, w_gate_up[0],
                   b_gate_up[0], w_down[0], b_down[0])
    yp, ys = _combine(dest, h, route, ln2_g[0].astype(F32).reshape(1, d),
                      ln2_b[0].astype(F32).reshape(1, d), yb, alpha, n_sample)

    y_prompt = yp.reshape(batch, seq, d).astype(x_prompt.dtype)
    y_sample = ys.reshape(n_sample, 1, d).astype(x_sample.dtype)
    new_conv_p = jnp.stack([proj[(b + 1) * seq - 3:(b + 1) * seq] for b in range(batch)])[None]
    new_conv_s = jnp.concatenate([conv_state[:, 1:, :], proj[n_prompt:, None, :]], axis=1)[None]
    sdt = state_gdn_s.dtype
    return (y_prompt, y_sample,
            new_conv_p.astype(state_gdn_conv.dtype), gdn_s_prompt[None].astype(sdt),
            hg_s_prompt[None].astype(state_hgrn_s.dtype),
            new_conv_s.astype(state_gdn_conv.dtype), gdn_s_sample[None].astype(sdt),
            hg_s_sample[None].astype(state_hgrn_s.dtype))
```

```python
import functools

import jax
import jax.numpy as jnp
from jax import lax
from jax.experimental import pallas as pl
from jax.experimental.pallas import tpu as pltpu

F32 = jnp.float32
BF16 = jnp.bfloat16
I32 = jnp.int32
U32 = jnp.uint32

HEAD_DIM = 128
LANES = 128
GDN_CHUNK = 64
HG_SUB = 16
TIME_BLOCK = 256
GDN_HEADS_PER_STEP = 4
HG_HEADS_PER_STEP = 4
STEP_UNROLL = 4
TOP_K = 4
MOE_ROW_BLOCK = 256
MOE_ROW_TILE = 1280
MOE_ALIGN_SLACK_BLOCKS = 16
MOE_VMEM_LIMIT = 60 * 1024 * 1024
MOE_COL_TILE = 512
SWIGLU_LIMIT = 7.0
SWIGLU_ALPHA = 1.702
LN_EPS = 1e-5
RMS_EPS = 1e-6
L2_EPS = 1e-6
VMEM_LIMIT = 56 * 1024 * 1024


def _params(n_axes, vmem=VMEM_LIMIT):
    return pltpu.CompilerParams(dimension_semantics=("arbitrary",) * n_axes,
                                vmem_limit_bytes=vmem)


def _pick(n, target, mult):
    best = None
    for d in range(mult, min(n, target) + 1, mult):
        if n % d == 0:
            best = d
    assert best is not None, (n, target, mult)
    return best


def _dot(a, b):
    return jnp.dot(a, b, preferred_element_type=F32)


def _dot_nt(a, b):
    return lax.dot_general(a, b, (((1,), (1,)), ((), ())), preferred_element_type=F32)


def _dot_tn(a, b):
    return lax.dot_general(a, b, (((0,), (0,)), ((), ())), preferred_element_type=F32)


def _hi_lo(x):
    hi = x.astype(BF16)
    lo = (x - hi.astype(F32)).astype(BF16)
    return hi, lo


def _dot3(a, b):
    ah, al = _hi_lo(a)
    bh, bl = _hi_lo(b)
    return _dot(ah, bh) + (_dot(ah, bl) + _dot(al, bh))


def _dot_exact_lhs(m_bf16, x):
    p1 = x.astype(BF16)
    r1 = x - p1.astype(F32)
    p2 = r1.astype(BF16)
    p3 = (r1 - p2.astype(F32)).astype(BF16)
    return _dot(m_bf16, p1) + (_dot(m_bf16, p2) + _dot(m_bf16, p3))


def _sigmoid(x):
    return 1.0 / (1.0 + jnp.exp(-x))


def _silu(x):
    return x * _sigmoid(x)


def _softplus(x):
    return jnp.maximum(x, 0.0) + jnp.log(1.0 + jnp.exp(-jnp.abs(x)))


def _iota(shape, dim):
    return lax.broadcasted_iota(I32, shape, dim)


def _layer_norm(y, g, b):
    mu = jnp.mean(y, axis=-1, keepdims=True)
    yc = y - mu
    var = jnp.mean(yc * yc, axis=-1, keepdims=True)
    return yc * lax.rsqrt(var + LN_EPS) * g + b


def _rms_gate(o, w, z):
    return o * lax.rsqrt(jnp.mean(o * o, axis=-1, keepdims=True) + RMS_EPS) * w * _silu(z)


def _l2norm(x):
    return x * lax.rsqrt(jnp.sum(x * x, axis=-1, keepdims=True) + L2_EPS)


def _lane_pick(x, idx):
    lane = _iota(x.shape, 1)
    return jnp.sum(jnp.where(lane == idx, x, 0.0), axis=1, keepdims=True)


def _mm_nt_kernel(x_ref, wt_ref, o_ref):
    o_ref[...] = _dot_nt(x_ref[...].astype(BF16), wt_ref[...].astype(BF16))


def _in_projection(x, wt, row0, no, tn_target):
    n, d = x.shape
    tm = _pick(n, 1040, 16)
    tn = _pick(no, tn_target, LANES)
    if row0 % tn == 0:
        wspec = pl.BlockSpec((tn, d), lambda i, j: (row0 // tn + j, 0))
    else:
        assert row0 % 8 == 0
        wspec = pl.BlockSpec((pl.Element(tn), pl.Element(d)),
                             lambda i, j: (pl.multiple_of(row0 + j * tn, 8), 0))
    return pl.pallas_call(
        _mm_nt_kernel,
        grid=(n // tm, no // tn),
        in_specs=[pl.BlockSpec((tm, d), lambda i, j: (i, 0)), wspec],
        out_specs=pl.BlockSpec((tm, tn), lambda i, j: (i, j)),
        out_shape=jax.ShapeDtypeStruct((n, no), F32),
        compiler_params=_params(2),
        name="in_projection",
    )(x, wt)


def _gates_kernel(x_ref, w_ref, prm_ref, g_ref, gt_ref, *, n_prompt_tiles, n_heads):
    i = pl.program_id(0)
    xh, xl = _hi_lo(x_ref[...])
    wh, wl = _hi_lo(w_ref[...])
    ab = _dot_nt(xh, wh) + (_dot_nt(xh, wl) + _dot_nt(xl, wh))
    prm = prm_ref[...]
    g = -jnp.exp(prm[0:1]) * _softplus(ab + prm[1:2])
    beta = _sigmoid(ab)
    tm = ab.shape[0]
    r = _iota((tm, tm), 0)
    c = _iota((tm, tm), 1)
    shift = jnp.where(i < n_prompt_tiles, GDN_CHUNK.bit_length() - 1, 0)
    tri = jnp.logical_and((r >> shift) == (c >> shift), r >= c)
    gc = _dot_exact_lhs(jnp.where(tri, 1.0, 0.0).astype(BF16), g)
    lane = _iota(ab.shape, 1)
    out = jnp.where(lane < n_heads, gc, beta)
    g_ref[...] = out
    gt_ref[...] = out.T


def _gates(x, w_ab, prm, n_prompt, n_heads):
    n, d = x.shape
    tm = LANES
    assert n % tm == 0 and n_prompt % tm == 0 and tm % GDN_CHUNK == 0
    kern = functools.partial(_gates_kernel, n_prompt_tiles=n_prompt // tm, n_heads=n_heads)
    return pl.pallas_call(
        kern,
        grid=(n // tm,),
        in_specs=[pl.BlockSpec((tm, d), lambda i: (i, 0)),
                  pl.BlockSpec((LANES, d), lambda i: (0, 0)),
                  pl.BlockSpec((8, LANES), lambda i: (0, 0))],
        out_specs=[pl.BlockSpec((tm, LANES), lambda i: (i, 0)),
                   pl.BlockSpec((LANES, tm), lambda i: (0, i))],
        out_shape=[jax.ShapeDtypeStruct((n, LANES), F32),
                   jax.ShapeDtypeStruct((LANES, n), F32)],
        compiler_params=_params(1),
        name="gates",
    )(x, w_ab, prm)


def _split_dot3(ah, al, bh, bl):
    return _dot(ah, bh) + (_dot(ah, bl) + _dot(al, bh))


def _unit_lower_inverses(mats):
    c = mats[0].shape[0]
    r = _iota((c, c), 0)
    col = _iota((c, c), 1)
    eye = jnp.where(r == col, 1.0, 0.0)
    pair = jnp.logical_and((r >> 1) == (col >> 1), r > col)
    invs = [eye - jnp.where(pair, a, 0.0) for a in mats]
    level = 2
    while (1 << level) <= c:
        half = level - 1
        mask = jnp.logical_and(
            (r >> level) == (col >> level),
            jnp.logical_and(((r >> half) & 1) == 1, ((col >> half) & 1) == 0))
        lows = [_hi_lo(jnp.where(mask, a, 0.0)) for a in mats]
        inv_s = [_hi_lo(inv) for inv in invs]
        xs = [_split_dot3(lh, ll, ih, il) for (lh, ll), (ih, il) in zip(lows, inv_s)]
        x_s = [_hi_lo(x) for x in xs]
        invs = [inv - _split_dot3(ih, il, xh, xl)
                for inv, (ih, il), (xh, xl) in zip(invs, inv_s, x_s)]
        level += 1
    return invs


def _gdn_prompt_kernel(pq_ref, pk_ref, pv_ref, pz_ref, g_ref, gt_ref, wq_ref, wk_ref, wv_ref,
                       nw_ref, o_ref, s_ref, state, cbuf, qs, ks, vs):
    hg = pl.program_id(1)
    t = pl.program_id(2)
    tb = pq_ref.shape[0]
    hb = state.shape[0]
    n_heads = pl.num_programs(1) * hb
    cl = GDN_CHUNK

    @pl.when(t == 0)
    def _():
        state[...] = jnp.zeros_like(state)
        cbuf[:, 0:8, :] = jnp.zeros((3, 8, hb * HEAD_DIM), F32)

    def conv(idx, u_ref, w_ref):
        u = u_ref[...]
        cbuf[idx, 8:8 + tb, :] = u
        w = w_ref[...]
        y = cbuf[idx, 5:5 + tb, :] * w[0:1, :]
        for j in range(1, 4):
            y = y + cbuf[idx, 5 + j:5 + j + tb, :] * w[j:j + 1, :]
        cbuf[idx, 0:8, :] = u[tb - 8:tb, :]
        return _silu(y)

    qc = conv(0, pq_ref, wq_ref)
    kc = conv(1, pk_ref, wk_ref)
    vs[...] = conv(2, pv_ref, wv_ref)
    for i in range(hb):
        sl = slice(i * HEAD_DIM, (i + 1) * HEAD_DIM)
        qs[:, sl] = _l2norm(qc[:, sl]) * (HEAD_DIM ** -0.5)
        ks[:, sl] = _l2norm(kc[:, sl])

    gall = g_ref[...]
    gt = gt_ref[...]
    gc_cols = [_lane_pick(gall, hg * hb + i) for i in range(hb)]
    beta_cols = [_lane_pick(gall, n_heads + hg * hb + i) for i in range(hb)]

    r = _iota((cl, cl), 0)
    col = _iota((cl, cl), 1)
    incl = r >= col
    strict = r > col

    pairs = [(c, i) for c in range(tb // cl) for i in range(hb)]
    pre = []
    for c, i in pairs:
        rows = slice(c * cl, (c + 1) * cl)
        sl = slice(i * HEAD_DIM, (i + 1) * HEAD_DIM)
        q = qs[rows, sl]
        k = ks[rows, sl]
        v = vs[rows, sl]
        gcc = gc_cols[i][rows, :]
        gcr = gt[i:i + 1, rows]
        bc = beta_cols[i][rows, :]
        decay = jnp.where(incl, jnp.exp(jnp.where(incl, gcc - gcr, 0.0)), 0.0)
        kb = k.astype(BF16)
        a = jnp.where(strict, bc * decay * _dot_nt(kb, kb), 0.0)
        aqk = (_dot_nt(q.astype(BF16), kb) * decay).astype(BF16)
        egc = jnp.exp(gcc)
        g_last = gcc[cl - 1:cl, :]
        pre.append(dict(a=a, aqk=aqk, rhs=_hi_lo(jnp.concatenate([bc * egc * k, bc * v], axis=1)),
                        qd=(q * egc).astype(BF16), ke=(k * jnp.exp(g_last - gcc)).astype(BF16),
                        g_end=jnp.exp(g_last)))
    invs = _unit_lower_inverses([p["a"] for p in pre])
    for p, inv in zip(pre, invs):
        ih, il = _hi_lo(inv)
        wu = _split_dot3(ih, il, *p["rhs"])
        p["w"] = wu[:, :HEAD_DIM].astype(BF16)
        p["u0"] = wu[:, HEAD_DIM:]

    s = [state[i] for i in range(hb)]
    outs = [[] for _ in range(hb)]
    for c in range(tb // cl):
        ps = [pre[c * hb + i] for i in range(hb)]
        sb = [x.astype(BF16) for x in s]
        ws = [_dot(p["w"], b) for p, b in zip(ps, sb)]
        qsd = [_dot(p["qd"], b) for p, b in zip(ps, sb)]
        ub = [(p["u0"] - x).astype(BF16) for p, x in zip(ps, ws)]
        s = [p["g_end"] * x + _dot_tn(p["ke"], u) for p, x, u in zip(ps, s, ub)]
        for i in range(hb):
            outs[i].append(qsd[i] + _dot(ps[i]["aqk"], ub[i]))
    nw = nw_ref[...]
    for i in range(hb):
        sl = slice(i * HEAD_DIM, (i + 1) * HEAD_DIM)
        state[i] = s[i]
        o_ref[:, sl] = _rms_gate(jnp.concatenate(outs[i], axis=0), nw, pz_ref[:, sl])

    @pl.when(t == pl.num_programs(2) - 1)
    def _():
        for i in range(hb):
            s_ref[i] = s[i]


def _gdn_prompt(proj, projb, gates, gates_t, conv_w, norm_w, batch, seq, n_heads):
    tb = _pick(seq, TIME_BLOCK, GDN_CHUNK)
    nt = seq // tb
    hb = _pick(n_heads, GDN_HEADS_PER_STEP, 1)
    ng = n_heads // hb
    d = n_heads * HEAD_DIM
    n_rows = gates_t.shape[1]
    gates_t3 = gates_t[:n_heads].reshape(ng, hb, n_rows)

    def rows(b, h, t):
        return b * nt + t

    def pspec(seg):
        return pl.BlockSpec((tb, hb * HEAD_DIM), lambda b, h, t: (rows(b, h, t), seg * ng + h))

    def wspec(seg):
        return pl.BlockSpec((4, hb * HEAD_DIM), lambda b, h, t: (0, seg * ng + h))

    return pl.pallas_call(
        _gdn_prompt_kernel,
        grid=(batch, ng, nt),
        in_specs=[pspec(0), pspec(1), pspec(2), pspec(0),
                  pl.BlockSpec((tb, LANES), lambda b, h, t: (rows(b, h, t), 0)),
                  pl.BlockSpec((None, hb, tb), lambda b, h, t: (h, 0, rows(b, h, t))),
                  wspec(0), wspec(1), wspec(2),
                  pl.BlockSpec((1, HEAD_DIM), lambda b, h, t: (0, 0))],
        out_specs=[pl.BlockSpec((tb, hb * HEAD_DIM), lambda b, h, t: (rows(b, h, t), h)),
                   pl.BlockSpec((None, hb, HEAD_DIM, HEAD_DIM), lambda b, h, t: (b, h, 0, 0))],
        out_shape=[jax.ShapeDtypeStruct((batch * seq, d), F32),
                   jax.ShapeDtypeStruct((batch, n_heads, HEAD_DIM, HEAD_DIM), F32)],
        scratch_shapes=[pltpu.VMEM((hb, HEAD_DIM, HEAD_DIM), F32),
                        pltpu.VMEM((3, tb + 8, hb * HEAD_DIM), F32),
                        pltpu.VMEM((tb, hb * HEAD_DIM), F32),
                        pltpu.VMEM((tb, hb * HEAD_DIM), F32),
                        pltpu.VMEM((tb, hb * HEAD_DIM), F32)],
        compiler_params=_params(3),
        name="gdn_prompt",
    )(proj, proj, proj, projb, gates, gates_t3, conv_w, conv_w, conv_w, norm_w)


def _lower_bound(logits):
    m = jnp.max(logits, axis=0, keepdims=True)
    e = jnp.exp(logits - m)
    return e[0:1, :] / jnp.sum(e, axis=0, keepdims=True)


def _hgrn_prompt_kernel(pq_ref, pf_ref, pi_ref, pz_ref, lb_ref, nw_ref, o_ref, s_ref,
                        state_t):
    t = pl.program_id(2)
    tb = pq_ref.shape[0]
    hb = state_t.shape[0]

    @pl.when(t == 0)
    def _():
        state_t[...] = jnp.zeros_like(state_t)

    lb = _lower_bound(lb_ref[...])
    f = lb + (1.0 - lb) * _sigmoid(pf_ref[...])
    kk = 1.0 - f
    lf = jnp.log2(f)
    q = _silu(pq_ref[...]) * (HEAD_DIM ** -0.5)
    v = pi_ref[...]

    r = _iota((tb, tb), 0)
    c = _iota((tb, tb), 1)
    sub_shift = HG_SUB.bit_length() - 1
    tri = jnp.logical_and((r >> sub_shift) == (c >> sub_shift), r >= c)
    b = _dot_exact_lhs(jnp.where(tri, 1.0, 0.0).astype(BF16), lf)

    width = hb * HEAD_DIM
    heads = [slice(i * HEAD_DIM, (i + 1) * HEAD_DIM) for i in range(hb)]

    half = HG_SUB // 2
    n_sub = tb // HG_SUB

    def halves(x):
        x4 = x.reshape(n_sub, 2, half, width)
        return x4[:, 0], x4[:, 1]

    q_lo, q_hi = halves(q)
    b_lo, b_hi = halves(b)
    k_lo, k_hi = halves(kk)
    v_lo, v_hi = halves(v)
    row = _iota((1, half, 1), 1)

    def rot(x, d):
        return x if d == 0 else pltpu.roll(x, d, 1)

    def add_terms(acc, qx, bx, kp, bp, vp, ok):
        prod = qx * kp * jnp.exp2(bx - bp)
        for i, sl in enumerate(heads):
            w = jnp.sum(prod[:, :, sl], axis=2, keepdims=True)
            if ok is not None:
                w = jnp.where(ok, w, 0.0)
            acc[i] = acc[i] + w * vp[:, :, sl]

    acc_lo = [jnp.zeros((n_sub, half, HEAD_DIM), F32) for _ in range(hb)]
    acc_hi = [jnp.zeros((n_sub, half, HEAD_DIM), F32) for _ in range(hb)]
    for d in range(half):
        ok = None if d == 0 else row >= d
        kl, bl_, vl = rot(k_lo, d), rot(b_lo, d), rot(v_lo, d)
        add_terms(acc_lo, q_lo, b_lo, kl, bl_, vl, ok)
        add_terms(acc_hi, q_hi, b_hi, kl, bl_, vl, ok)
        if d == 0:
            add_terms(acc_hi, q_hi, b_hi, k_hi, b_hi, v_hi, None)
        else:
            add_terms(acc_hi, q_hi, b_hi, jnp.where(ok, rot(k_hi, d), kl),
                      jnp.where(ok, rot(b_hi, d), bl_), jnp.where(ok, rot(v_hi, d), vl), None)
    o_intra = [jnp.stack([lo_, hi_], axis=1).reshape(tb, HEAD_DIM)
               for lo_, hi_ in zip(acc_lo, acc_hi)]

    n_sub = tb // HG_SUB
    eb = jnp.exp2(b)
    qe = (q * eb).astype(BF16)
    incs, scales = [], []
    for j in range(n_sub):
        rows = slice(j * HG_SUB, (j + 1) * HG_SUB)
        bl = b[(j + 1) * HG_SUB - 1:(j + 1) * HG_SUB, :]
        ke = (kk[rows, :] * jnp.exp2(bl - b[rows, :])).astype(BF16)
        vb = v[rows, :].astype(BF16)
        incs.append([_dot_tn(vb[:, sl], ke[:, sl]) for sl in heads])
        scales.append(eb[(j + 1) * HG_SUB - 1:(j + 1) * HG_SUB, :])
    st = [state_t[i] for i in range(hb)]
    before = []
    for j in range(n_sub):
        before.append([x.astype(BF16) for x in st])
        st = [x * scales[j][:, sl] + inc for x, sl, inc in zip(st, heads, incs[j])]
    nw = nw_ref[...]
    for i, sl in enumerate(heads):
        outs = [_dot_nt(qe[j * HG_SUB:(j + 1) * HG_SUB, sl], before[j][i]) for j in range(n_sub)]
        state_t[i] = st[i]
        o_ref[:, sl] = _rms_gate(o_intra[i] + jnp.concatenate(outs, axis=0), nw, pz_ref[:, sl])

    @pl.when(t == pl.num_programs(2) - 1)
    def _():
        for i in range(hb):
            s_ref[i] = st[i].T


def _hgrn_prompt(projb, lb_logits, norm_w, batch, seq, n_heads):
    tb = _pick(seq, TIME_BLOCK, HG_SUB)
    nt = seq // tb
    hb = _pick(n_heads, HG_HEADS_PER_STEP, 1)
    ng = n_heads // hb
    d = n_heads * HEAD_DIM
    n_lb = lb_logits.shape[0]
    width = hb * HEAD_DIM

    def pspec(seg):
        return pl.BlockSpec((tb, width), lambda b, h, t: (b * nt + t, seg * ng + h))

    return pl.pallas_call(
        _hgrn_prompt_kernel,
        grid=(batch, ng, nt),
        in_specs=[pspec(1), pspec(2), pspec(3), pspec(4),
                  pl.BlockSpec((n_lb, width), lambda b, h, t: (0, h)),
                  pl.BlockSpec((1, HEAD_DIM), lambda b, h, t: (0, 0))],
        out_specs=[pl.BlockSpec((tb, width), lambda b, h, t: (b * nt + t, h)),
                   pl.BlockSpec((None, hb, HEAD_DIM, HEAD_DIM), lambda b, h, t: (b, h, 0, 0))],
        out_shape=[jax.ShapeDtypeStruct((batch * seq, d), F32),
                   jax.ShapeDtypeStruct((batch, n_heads, HEAD_DIM, HEAD_DIM), F32)],
        scratch_shapes=[pltpu.VMEM((hb, HEAD_DIM, HEAD_DIM), F32)],
        compiler_params=_params(3),
        name="hgrn_prompt",
    )(projb, projb, projb, projb, lb_logits, norm_w)


def _sample_prep_kernel(pq_ref, pk_ref, pv_ref, cq_ref, ck_ref, cv_ref, wq_ref, wk_ref, wv_ref,
                        g_ref, hq_ref, hf_ref, lb_ref,
                        q_ref, k_ref, v_ref, eg_ref, beta_ref, f_ref, qh_ref):
    h = pl.program_id(0)

    def conv(u_ref, c_ref, w_ref):
        w = w_ref[...]
        y = u_ref[...] * w[3:4, :]
        for j in range(3):
            y = y + c_ref[:, j, :] * w[j:j + 1, :]
        return _silu(y)

    q_ref[...] = _l2norm(conv(pq_ref, cq_ref, wq_ref)) * (HEAD_DIM ** -0.5)
    k_ref[...] = _l2norm(conv(pk_ref, ck_ref, wk_ref))
    v_ref[...] = conv(pv_ref, cv_ref, wv_ref)
    gall = g_ref[...]
    shape = q_ref.shape
    eg_ref[...] = jnp.broadcast_to(jnp.exp(_lane_pick(gall, h)), shape)
    beta_ref[...] = jnp.broadcast_to(_lane_pick(gall, h + pl.num_programs(0)), shape)
    lb = _lower_bound(lb_ref[...])
    f_ref[...] = lb + (1.0 - lb) * _sigmoid(hf_ref[...])
    qh_ref[...] = _silu(hq_ref[...]) * (HEAD_DIM ** -0.5)


def _sample_prep(proj, projb, conv_state, conv_w, gates, lb_logits, n_prompt, n_sample, n_heads):
    assert n_prompt % n_sample == 0
    rb = n_prompt // n_sample
    d = n_heads * HEAD_DIM
    n_lb = lb_logits.shape[0]

    def pspec(seg):
        return pl.BlockSpec((n_sample, HEAD_DIM), lambda h: (rb, seg * n_heads + h))

    def cspec(seg):
        return pl.BlockSpec((n_sample, 3, HEAD_DIM), lambda h: (0, 0, seg * n_heads + h))

    def wspec(seg):
        return pl.BlockSpec((4, HEAD_DIM), lambda h: (0, seg * n_heads + h))

    ospec = pl.BlockSpec((n_sample, HEAD_DIM), lambda h: (0, h))
    oshape = jax.ShapeDtypeStruct((n_sample, d), F32)
    return pl.pallas_call(
        _sample_prep_kernel,
        grid=(n_heads,),
        in_specs=[pspec(0), pspec(1), pspec(2), cspec(0), cspec(1), cspec(2),
                  wspec(0), wspec(1), wspec(2),
                  pl.BlockSpec((n_sample, LANES), lambda h: (rb, 0)),
                  pspec(1), pspec(2),
                  pl.BlockSpec((n_lb, HEAD_DIM), lambda h: (0, h))],
        out_specs=[ospec] * 7,
        out_shape=[oshape] * 7,
        compiler_params=_params(1),
        name="sample_prep",
    )(proj, proj, proj, conv_state, conv_state, conv_state, conv_w, conv_w, conv_w,
      gates, projb, projb, lb_logits)


def _gdn_step_kernel(s_ref, qt_ref, kt_ref, v_ref, eg_ref, beta_ref, pz_ref, nw_ref,
                     so_ref, o_ref, obuf):
    i = pl.program_id(0)
    bt = v_ref.shape[0]
    qt = qt_ref[...]
    kt = kt_ref[...]

    def body(bb, carry):
        bg = i * bt + bb
        kcol = _lane_pick(kt, bg)
        qcol = _lane_pick(qt, bg)
        sd = s_ref[bb] * eg_ref[pl.ds(bb, 1), :]
        ks = jnp.sum(sd * kcol, axis=0, keepdims=True)
        u = beta_ref[pl.ds(bb, 1), :] * (v_ref[pl.ds(bb, 1), :] - ks)
        sn = sd + kcol * u
        so_ref[bb] = sn
        obuf[pl.ds(bb, 1), :] = jnp.sum(sn * qcol, axis=0, keepdims=True)
        return carry

    lax.fori_loop(0, bt, body, 0, unroll=STEP_UNROLL)
    o_ref[...] = _rms_gate(obuf[...], nw_ref[...], pz_ref[...])


def _hgrn_step_kernel(s_ref, qt_ref, ft_ref, pi_ref, pz_ref, nw_ref, so_ref, o_ref, obuf):
    i = pl.program_id(0)
    bt = pi_ref.shape[0]
    qt = qt_ref[...]
    ft = ft_ref[...]

    def body(bb, carry):
        bg = i * bt + bb
        fcol = _lane_pick(ft, bg)
        qcol = _lane_pick(qt, bg)
        sn = fcol * s_ref[bb] + (1.0 - fcol) * pi_ref[pl.ds(bb, 1), :]
        so_ref[bb] = sn
        obuf[pl.ds(bb, 1), :] = jnp.sum(sn * qcol, axis=0, keepdims=True)
        return carry

    lax.fori_loop(0, bt, body, 0, unroll=STEP_UNROLL)
    o_ref[...] = _rms_gate(obuf[...], nw_ref[...], pz_ref[...])


def _sample_step(kind, state, cols_t, rows, projb, norm_w, n_prompt, n_heads):
    n_sample = state.shape[0]
    bt = _pick(n_sample, 16, 8)
    rb = n_prompt // bt
    sspec = pl.BlockSpec((bt, None, HEAD_DIM, HEAD_DIM), lambda i, h: (i, h, 0, 0))
    tspec = pl.BlockSpec((HEAD_DIM, n_sample), lambda i, h: (h, 0))
    rspec = pl.BlockSpec((bt, HEAD_DIM), lambda i, h: (i, h))

    def pspec(seg):
        return pl.BlockSpec((bt, HEAD_DIM), lambda i, h: (rb + i, seg * n_heads + h))

    nspec = pl.BlockSpec((1, HEAD_DIM), lambda i, h: (0, 0))
    if kind == "gdn":
        kern = _gdn_step_kernel
        in_specs = [sspec, tspec, tspec, rspec, rspec, rspec, pspec(0), nspec]
        args = [state, *cols_t, *rows, projb, norm_w]
    else:
        kern = _hgrn_step_kernel
        in_specs = [sspec, tspec, tspec, pspec(3), pspec(4), nspec]
        args = [state, *cols_t, projb, projb, norm_w]
    return pl.pallas_call(
        kern,
        grid=(n_sample // bt, n_heads),
        in_specs=in_specs,
        out_specs=[sspec, rspec],
        out_shape=[jax.ShapeDtypeStruct(state.shape, F32),
                   jax.ShapeDtypeStruct((n_sample, n_heads * HEAD_DIM), F32)],
        scratch_shapes=[pltpu.VMEM((bt, HEAD_DIM), F32)],
        compiler_params=_params(2),
        name=kind + "_step",
    )(*args)


def _out_proj_kernel(ra_ref, rb_ref, oa_ref, oas_ref, ob_ref, obs_ref, x_ref, w_ref, g_ref, b_ref,
                     h_ref, *, alpha, n_prompt_tiles):
    is_prompt = pl.program_id(0) < n_prompt_tiles
    oa = jnp.where(is_prompt, oa_ref[...], oas_ref[...])
    ob = jnp.where(is_prompt, ob_ref[...], obs_ref[...])
    merged = _sigmoid(ra_ref[...]) * oa + _sigmoid(rb_ref[...]) * ob
    mix = _dot(merged.astype(BF16), w_ref[...])
    h_ref[...] = _layer_norm(alpha * x_ref[...] + mix, g_ref[...], b_ref[...])


def _out_projection(projb, oa, oa_s, ob, ob_s, x, w_out, ln_g, ln_b, alpha):
    n, d = x.shape
    n_prompt = oa.shape[0]
    tm = oa_s.shape[0]
    assert n_prompt % tm == 0 and n == n_prompt + tm
    npt = n_prompt // tm
    row = pl.BlockSpec((tm, d), lambda i: (i, 0))
    prow = pl.BlockSpec((tm, d), lambda i: (jnp.minimum(i, npt - 1), 0))
    srow = pl.BlockSpec((tm, d), lambda i: (0, 0))
    vec = pl.BlockSpec((1, d), lambda i: (0, 0))
    return pl.pallas_call(
        functools.partial(_out_proj_kernel, alpha=alpha, n_prompt_tiles=npt),
        grid=(n // tm,),
        in_specs=[pl.BlockSpec((tm, d), lambda i: (i, 5)),
                  pl.BlockSpec((tm, d), lambda i: (i, 6)),
                  prow, srow, prow, srow, row,
                  pl.BlockSpec((d, d), lambda i: (0, 0)), vec, vec],
        out_specs=row,
        out_shape=jax.ShapeDtypeStruct((n, d), F32),
        compiler_params=_params(1),
        name="out_projection",
    )(projb, projb, oa, oa_s, ob, ob_s, x, w_out, ln_g, ln_b)


def _router_kernel(h_ref, w_ref, b_ref, r_ref, cnt_ref, carry, *, n_experts):
    i = pl.program_id(0)

    @pl.when(i == 0)
    def _():
        carry[...] = jnp.zeros_like(carry)

    logits = _dot3(h_ref[...], w_ref[...]) + b_ref[...]
    tm = logits.shape[0]
    lane = _iota(logits.shape, 1)
    x = jnp.where(lane < n_experts, logits, -jnp.inf)
    vals, idxs = [], []
    for _ in range(TOP_K):
        m = jnp.max(x, axis=1, keepdims=True)
        idx = jnp.min(jnp.where(x == m, lane, LANES), axis=1, keepdims=True)
        vals.append(m)
        idxs.append(idx)
        x = jnp.where(lane == idx, -jnp.inf, x)
    es = [jnp.exp(v - vals[0]) for v in vals]
    den = es[0] + es[1] + es[2] + es[3]
    hot = jnp.zeros(logits.shape, F32)
    for idx in idxs:
        hot = hot + jnp.where(lane == idx, 1.0, 0.0)
    r = _iota((tm, tm), 0)
    c = _iota((tm, tm), 1)
    before = _dot(jnp.where(r > c, 1.0, 0.0).astype(BF16), hot.astype(BF16)) + carry[...]
    out = jnp.zeros(logits.shape, F32)
    for k in range(TOP_K):
        rank = jnp.sum(jnp.where(lane == idxs[k], before, 0.0), axis=1, keepdims=True)
        out = out + jnp.where(lane == k, idxs[k].astype(F32), 0.0)
        out = out + jnp.where(lane == TOP_K + k, rank, 0.0)
        out = out + jnp.where(lane == 2 * TOP_K + k, es[k] / den, 0.0)
    r_ref[...] = out
    carry[...] = carry[...] + jnp.sum(hot, axis=0, keepdims=True)
    cnt_ref[...] = carry[...]


def _router(h, w_r, b_r, n_experts):
    n, d = h.shape
    tm = _pick(n, 208, 16)
    return pl.pallas_call(
        functools.partial(_router_kernel, n_experts=n_experts),
        grid=(n // tm,),
        in_specs=[pl.BlockSpec((tm, d), lambda i: (i, 0)),
                  pl.BlockSpec((d, LANES), lambda i: (0, 0)),
                  pl.BlockSpec((1, LANES), lambda i: (0, 0))],
        out_specs=[pl.BlockSpec((tm, LANES), lambda i: (i, 0)),
                   pl.BlockSpec((1, LANES), lambda i: (0, 0))],
        out_shape=[jax.ShapeDtypeStruct((n, LANES), F32),
                   jax.ShapeDtypeStruct((1, LANES), F32)],
        scratch_shapes=[pltpu.VMEM((1, LANES), F32)],
        compiler_params=_params(1),
        name="router",
    )(h, w_r, b_r)


def _dispatch_kernel(dest_ref, h_ref, xin_ref, xb_ref, sem):
    del xin_ref
    i = pl.program_id(0)
    tm = h_ref.shape[0]

    def row_copy(r, k):
        dst = dest_ref[(i * tm + r) * TOP_K + k]
        return pltpu.make_async_copy(h_ref.at[pl.ds(r, 1), :], xb_ref.at[pl.ds(dst, 1), :], sem)

    def start(r, carry):
        for k in range(TOP_K):
            row_copy(r, k).start()
        return carry

    def wait(r, carry):
        for k in range(TOP_K):
            row_copy(r, k).wait()
        return carry

    lax.fori_loop(0, tm, start, 0, unroll=8)
    lax.fori_loop(0, tm, wait, 0, unroll=8)


def _dispatch(dest_flat, h, xb_init):
    n, d = h.shape
    tm = _pick(n, 128, 8)
    return pl.pallas_call(
        _dispatch_kernel,
        grid_spec=pltpu.PrefetchScalarGridSpec(
            num_scalar_prefetch=1,
            grid=(n // tm,),
            in_specs=[pl.BlockSpec((tm, d), lambda i, dest: (i, 0)),
                      pl.BlockSpec(memory_space=pl.ANY)],
            out_specs=pl.BlockSpec(memory_space=pl.ANY),
            scratch_shapes=[pltpu.SemaphoreType.DMA(())]),
        out_shape=jax.ShapeDtypeStruct(xb_init.shape, F32),
        input_output_aliases={2: 0},
        compiler_params=_params(1),
        name="dispatch",
    )(dest_flat, h, xb_init)


def _swiglu_interleaved(h):
    m = h.shape[0]
    lane = _iota((m, LANES), 1)
    low = lane < LANES // 2
    evens_first = jnp.where(low, 2 * lane, 2 * lane - (LANES - 1))
    acts = []
    for p in range(h.shape[1] // (2 * LANES)):
        a = jnp.take_along_axis(h[:, 2 * p * LANES:(2 * p + 1) * LANES], evens_first, axis=1)
        b = jnp.take_along_axis(h[:, (2 * p + 1) * LANES:(2 * p + 2) * LANES], evens_first, axis=1)
        gate = jnp.where(low, a, pltpu.roll(b, LANES // 2, 1))
        up = jnp.where(low, pltpu.roll(a, LANES // 2, 1), b)
        gate = jnp.minimum(gate, SWIGLU_LIMIT)
        up = jnp.clip(up, -SWIGLU_LIMIT, SWIGLU_LIMIT)
        acts.append((up + 1.0) * (gate * _sigmoid(gate * SWIGLU_ALPHA)))
    return jnp.concatenate(acts, axis=1)


def _moe_kernel(tile_ref, exp_ref, blk0_ref, nblk_ref, zero_ref, x_ref, wgu_ref, bgu_ref, wdn_ref,
                bdn_ref, y_ref):
    del tile_ref, exp_ref
    it = pl.program_id(0)
    j = pl.program_id(1)
    blk0 = blk0_ref[it]
    nblk = nblk_ref[it]

    @pl.when(jnp.logical_and(j == 0, zero_ref[it] == 1))
    def _():
        y_ref[...] = jnp.zeros_like(y_ref)

    def run_blocks(n_blocks):
        wg = wgu_ref[...].astype(BF16)
        wd = wdn_ref[...].astype(BF16)
        bgu = bgu_ref[...]
        bdn = bdn_ref[...]
        hs = {}

        def rows_of(i):
            return pl.ds(pl.multiple_of((blk0 + i) * MOE_ROW_BLOCK, MOE_ROW_BLOCK), MOE_ROW_BLOCK)

        def up_proj(i):
            hs[i] = _dot(x_ref[rows_of(i), :].astype(BF16), wg) + bgu

        def down_proj(i):
            contrib = _dot(_swiglu_interleaved(hs.pop(i)).astype(BF16), wd)
            prev = jnp.where(j == 0, jnp.broadcast_to(bdn, contrib.shape), y_ref[rows_of(i), :])
            y_ref[rows_of(i), :] = prev + contrib

        up_proj(0)
        for i in range(1, n_blocks):
            up_proj(i)
            down_proj(i - 1)
        down_proj(n_blocks - 1)

    for n_blocks in range(1, MOE_ROW_TILE // MOE_ROW_BLOCK + 1):
        pl.when(nblk == n_blocks)(functools.partial(run_blocks, n_blocks))


def _moe_gemm(item_tile, item_exp, item_blk0, item_nblk, item_zero, xb, w_gu, b_gu, w_dn, b_dn):
    p_rows = xb.shape[0]
    n_exp, d, two_de = w_gu.shape
    de = two_de // 2
    n_items = item_tile.shape[0]
    nj = two_de // MOE_COL_TILE
    dn_rows = MOE_COL_TILE // 2

    def live_j(it, j, nblk):
        return jnp.where(nblk[it] > 0, j, nj - 1)

    return pl.pallas_call(
        _moe_kernel,
        grid_spec=pltpu.PrefetchScalarGridSpec(
            num_scalar_prefetch=5,
            grid=(n_items, nj),
            in_specs=[
                pl.BlockSpec((MOE_ROW_TILE, d), lambda it, j, tl, ex, b0, nb, zf: (tl[it], 0)),
                pl.BlockSpec((None, d, MOE_COL_TILE),
                             lambda it, j, tl, ex, b0, nb, zf: (ex[it], 0, live_j(it, j, nb))),
                pl.BlockSpec((None, 1, MOE_COL_TILE),
                             lambda it, j, tl, ex, b0, nb, zf: (ex[it], 0, live_j(it, j, nb))),
                pl.BlockSpec((None, dn_rows, d),
                             lambda it, j, tl, ex, b0, nb, zf: (ex[it], live_j(it, j, nb), 0)),
                pl.BlockSpec((None, 1, d), lambda it, j, tl, ex, b0, nb, zf: (ex[it], 0, 0)),
            ],
            out_specs=pl.BlockSpec((MOE_ROW_TILE, d), lambda it, j, tl, ex, b0, nb, zf: (tl[it], 0))),
        out_shape=jax.ShapeDtypeStruct((p_rows, d), F32),
        compiler_params=_params(2, MOE_VMEM_LIMIT),
        name="moe_gemm",
    )(item_tile, item_exp, item_blk0, item_nblk, item_zero, xb, w_gu, b_gu.reshape(n_exp, 1, two_de),
      w_dn, b_dn.reshape(n_exp, 1, d))


def _combine_kernel(dest_ref, h_ref, r_ref, g_ref, b_ref, yb_ref, op_ref, os_ref, rows, sem, *,
                    alpha, n_prompt_tiles):
    i = pl.program_id(0)
    tm, d = h_ref.shape

    def row_copy(r, k):
        src = dest_ref[(i * tm + r) * TOP_K + k]
        return pltpu.make_async_copy(yb_ref.at[pl.ds(src, 1), :], rows.at[k, pl.ds(r, 1), :], sem)

    def start(r, carry):
        for k in range(TOP_K):
            row_copy(r, k).start()
        return carry

    def wait(r, carry):
        for k in range(TOP_K):
            row_copy(r, k).wait()
        return carry

    lax.fori_loop(0, tm, start, 0, unroll=8)
    lax.fori_loop(0, tm, wait, 0, unroll=8)
    rr = r_ref[...]
    ffn = jnp.zeros((tm, d), F32)
    for k in range(TOP_K):
        ffn = ffn + _lane_pick(rr, 2 * TOP_K + k) * rows[k]
    out = _layer_norm(alpha * h_ref[...] + ffn, g_ref[...], b_ref[...])

    @pl.when(i < n_prompt_tiles)
    def _():
        op_ref[...] = out

    @pl.when(i >= n_prompt_tiles)
    def _():
        os_ref[...] = out


def _combine(dest_flat, h, route, ln_g, ln_b, yb, alpha, n_sample):
    n, d = h.shape
    tm = n_sample
    n_prompt = n - n_sample
    assert n_prompt % tm == 0
    npt = n_prompt // tm
    return pl.pallas_call(
        functools.partial(_combine_kernel, alpha=alpha, n_prompt_tiles=npt),
        grid_spec=pltpu.PrefetchScalarGridSpec(
            num_scalar_prefetch=1,
            grid=(n // tm,),
            in_specs=[pl.BlockSpec((tm, d), lambda i, dest: (i, 0)),
                      pl.BlockSpec((tm, LANES), lambda i, dest: (i, 0)),
                      pl.BlockSpec((1, d), lambda i, dest: (0, 0)),
                      pl.BlockSpec((1, d), lambda i, dest: (0, 0)),
                      pl.BlockSpec(memory_space=pl.ANY)],
            out_specs=[pl.BlockSpec((tm, d), lambda i, dest: (jnp.minimum(i, npt - 1), 0)),
                       pl.BlockSpec((tm, d), lambda i, dest: (0, 0))],
            scratch_shapes=[pltpu.VMEM((TOP_K, tm, d), F32),
                            pltpu.SemaphoreType.DMA(())]),
        out_shape=[jax.ShapeDtypeStruct((n_prompt, d), F32),
                   jax.ShapeDtypeStruct((n_sample, d), F32)],
        compiler_params=_params(1),
        name="combine",
    )(dest_flat, h, route, ln_g, ln_b, yb)


def _routing_tables(route, counts_row, n_experts):
    n = route.shape[0]
    e_idx = route[:, 0:TOP_K].astype(I32)
    rank = route[:, TOP_K:2 * TOP_K].astype(I32)
    counts = counts_row[0, :n_experts].astype(I32)
    bpt = MOE_ROW_TILE // MOE_ROW_BLOCK
    blocks_e = (counts + MOE_ROW_BLOCK - 1) // MOE_ROW_BLOCK
    starts = []
    pos = jnp.int32(0)
    skipped = jnp.int32(0)
    for e in range(n_experts):
        off = pos % bpt
        gap = jnp.where((off > 0) & (off + blocks_e[e] > bpt), bpt - off, 0)
        gap = jnp.where(skipped + gap <= MOE_ALIGN_SLACK_BLOCKS, gap, 0)
        starts.append(pos + gap)
        pos = pos + gap + blocks_e[e]
        skipped = skipped + gap
    start_blk = jnp.stack(starts)
    end_blk = start_blk + blocks_e
    dest = (start_blk[e_idx] * MOE_ROW_BLOCK + rank).reshape(-1)

    max_blocks = -(-(n * TOP_K + n_experts * (MOE_ROW_BLOCK - 1)) // MOE_ROW_BLOCK)
    p_rows = -(-(max_blocks + MOE_ALIGN_SLACK_BLOCKS) // bpt) * MOE_ROW_TILE
    n_blocks = p_rows // MOE_ROW_BLOCK
    max_items = p_rows // MOE_ROW_TILE + n_experts
    blk = jnp.arange(n_blocks, dtype=I32)
    inside = (blk[:, None] >= start_blk[None, :]) & (blk[:, None] < end_blk[None, :])
    valid = jnp.any(inside, axis=1)
    blk_e = jnp.sum(jnp.where(inside, jnp.arange(n_experts, dtype=I32)[None, :], 0), axis=1)
    first = valid & ((blk % bpt == 0) | (blk_e != jnp.roll(blk_e, 1)) | ~jnp.roll(valid, 1))
    item_of_blk = jnp.cumsum(first.astype(I32)) - 1
    n_items = jnp.sum(first.astype(I32))
    it = jnp.arange(max_items, dtype=I32)
    mine = item_of_blk[None, :] == it[:, None]
    item_first = jnp.sum(jnp.where(mine & first[None, :], blk[None, :], 0), axis=1)
    item_nblk = jnp.sum((mine & valid[None, :]).astype(I32), axis=1)
    live = it < n_items
    item_first = item_first[jnp.minimum(it, n_items - 1)]
    item_nblk = jnp.where(live, item_nblk, 0)
    item_blk0 = item_first % bpt
    n_tiles = p_rows // MOE_ROW_TILE
    last_tile = item_first[max_items - 1] // bpt
    idle_tile = last_tile + 1 + (it - n_items)
    item_tile = jnp.where(live, item_first // bpt, jnp.minimum(idle_tile, n_tiles - 1))
    item_zero = jnp.where(live, (item_blk0 == 0) & (item_nblk > 0), idle_tile < n_tiles)
    return (dest, p_rows, item_tile, blk_e[item_first], item_blk0, item_nblk,
            item_zero.astype(I32))


def kernel(x_prompt, x_sample, state_gdn_conv, state_gdn_s, state_hgrn_s, w_in, gdn_conv_w,
           gdn_a_log, gdn_dt_bias, gdn_norm_w, hg_lb_logits, hg_norm_w, w_out, ln1_g, ln1_b,
           w_router, b_router, w_gate_up, b_gate_up, w_down, b_down, ln2_g, ln2_b):
    depth = w_in.shape[0]
    assert depth == 1
    batch, seq, d = x_prompt.shape
    n_sample = x_sample.shape[0]
    assert x_sample.shape[1] == 1
    n_heads = d // HEAD_DIM
    n_prompt = batch * seq
    n_rows = n_prompt + n_sample
    n_experts = w_router.shape[-1]
    alpha = (2.0 * depth) ** 0.25
    qkv = 3 * d
    ab0 = qkv
    ab1 = qkv + 2 * n_heads

    x_all = jnp.concatenate([x_prompt.reshape(n_prompt, d), x_sample.reshape(n_sample, d)],
                            axis=0).astype(F32)
    wt = jnp.swapaxes(w_in[0], 0, 1).astype(F32)
    w_ab = jnp.pad(wt[ab0:ab1], ((0, LANES - 2 * n_heads), (0, 0)))
    prm = jnp.zeros((8, LANES), F32)
    prm = prm.at[0, :n_heads].set(gdn_a_log[0].astype(F32))
    prm = prm.at[1, :n_heads].set(gdn_dt_bias[0].astype(F32))
    conv_w = gdn_conv_w[0].astype(F32)
    gdn_nw = gdn_norm_w[0].astype(F32).reshape(1, HEAD_DIM)
    hg_nw = hg_norm_w[0].astype(F32).reshape(1, HEAD_DIM)
    lb_logits = hg_lb_logits.astype(F32)

    x_bf = x_all.astype(BF16)
    proj = _in_projection(x_bf, wt, 0, qkv, 1024)
    projb = _in_projection(x_bf, wt, ab1, wt.shape[0] - ab1, 1024)
    gates, gates_t = _gates(x_all, w_ab, prm, n_prompt, n_heads)
    oa, gdn_s_prompt = _gdn_prompt(proj, projb, gates, gates_t, conv_w, gdn_nw, batch, seq, n_heads)
    ob, hg_s_prompt = _hgrn_prompt(projb, lb_logits, hg_nw, batch, seq, n_heads)

    conv_state = state_gdn_conv[0].astype(F32)
    sq, sk, sv, seg, sbeta, sf, sqh = _sample_prep(proj, projb, conv_state, conv_w, gates,
                                                   lb_logits,
                                                   n_prompt, n_sample, n_heads)
    gdn_s_sample, oa_s = _sample_step("gdn", state_gdn_s[0].astype(F32), (sq.T, sk.T),
                                      (sv, seg, sbeta), projb, gdn_nw, n_prompt, n_heads)
    hg_s_sample, ob_s = _sample_step("hgrn", state_hgrn_s[0].astype(F32), (sqh.T, sf.T), (),
                                     projb, hg_nw, n_prompt, n_heads)

    h = _out_projection(projb, oa, oa_s, ob, ob_s, x_all, w_out[0].astype(BF16),
                        ln1_g[0].astype(F32).reshape(1, d), ln1_b[0].astype(F32).reshape(1, d),
                        alpha)

    w_r = jnp.pad(w_router[0].astype(F32), ((0, 0), (0, LANES - n_experts)))
    b_r = jnp.pad(b_router[0].astype(F32), (0, LANES - n_experts)).reshape(1, LANES)
    route, counts_row = _router(h, w_r, b_r, n_experts)
    dest, p_rows, item_tile, item_exp, item_blk0, item_nblk, item_zero = _routing_tables(
        route, counts_row, n_experts)

    xb = _dispatch(dest, h, jnp.zeros((p_rows, d), F32))
    yb = _moe_gemm(item_tile, item_exp, item_blk0, item_nblk, item_zero, xb, w_gate_up[0],
                   b_gate_up[0], w_down[0], b_down[0])
    yp, ys = _combine(dest, h, route, ln2_g[0].astype(F32).reshape(1, d),
                      ln2_b[0].astype(F32).reshape(1, d), yb, alpha, n_sample)

    y_prompt = yp.reshape(batch, seq, d).astype(x_prompt.dtype)
    y_sample = ys.reshape(n_sample, 1, d).astype(x_sample.dtype)
    new_conv_p = jnp.stack([proj[(b + 1) * seq - 3:(b + 1) * seq] for b in range(batch)])[None]
    new_conv_s = jnp.concatenate([conv_state[:, 1:, :], proj[n_prompt:, None, :]], axis=1)[None]
    sdt = state_gdn_s.dtype
    return (y_prompt, y_sample,
            new_conv_p.astype(state_gdn_conv.dtype), gdn_s_prompt[None].astype(sdt),
            hg_s_prompt[None].astype(state_hgrn_s.dtype),
            new_conv_s.astype(state_gdn_conv.dtype), gdn_s_sample[None].astype(sdt),
            hg_s_sample[None].astype(state_hgrn_s.dtype))
```
